```python
import math
import jax, jax.numpy as jnp
from jax import lax
import numpy as np

D_MODEL = 1024
BATCH = 8
SEQ = 4096
DEPTH = 2
DEC_BATCH = 32
DEC_SEQ = 1
PAST_LEN = 16384
PAGE_SIZE = 128

HEAD_DIM = 64
HA = 4
HB = 4
HC = 4
A_QK = HA * 2 * HEAD_DIM
A_V = HA * 2 * HEAD_DIM
B_W = HB * HEAD_DIM
C_W = HC * HEAD_DIM
MIX_WIDTH = A_V + B_W + C_W
N_BRANCH = 3
Q_COLS = A_QK + B_W + C_W
KV_WIDTH = A_QK + A_V + 2 * B_W + 2 * C_W
F_COL = Q_COLS + KV_WIDTH
G_COL = F_COL + HC
IN_COLS = G_COL + N_BRANCH * D_MODEL

MOBA_BLOCK = 256
MOBA_TOPK = 3
MOBA_Q_CHUNK = 32
ATTN_Q_BLOCK = 128

D_FF = 2816
N_EXPERTS = 8
TOP_K = 2
E_FF = 1408
N_DENSE = (DEPTH + 1) // 2
N_MOE = DEPTH // 2

DEEPNORM_ALPHA = (2 * DEPTH) ** 0.25
DEEPNORM_BETA = (8 * DEPTH) ** -0.25
LN_EPS = 1e-5
FORGET_BIAS = 2.0


def _offsets(widths):
    bounds = np.cumsum((0,) + widths)
    return tuple((int(a), int(b)) for a, b in zip(bounds[:-1], bounds[1:]))


KV_SEGS = _offsets((A_QK, A_V, B_W, B_W, C_W, C_W))

kernel_name = 'hybrid_diff_moba_fox_decoder_step'


def alibi_slopes():
    n = HA + HB
    return 2.0 ** (-8.0 * (jnp.arange(n, dtype=jnp.float32) + 1.0) / n)


def layer_norm(x, g, b):
    xf = x.astype(jnp.float32)
    mu = jnp.mean(xf, -1, keepdims=True)
    var = jnp.mean(jnp.square(xf - mu), -1, keepdims=True)
    return ((xf - mu) * lax.rsqrt(var + LN_EPS) * g + b).astype(x.dtype)


def ada_params(c, w, b):
    p = jax.nn.silu(c) @ w + b
    shift, scale, gate = jnp.split(p[:, None, :], 3, axis=-1)
    return shift, scale, gate


def swiglu(u, w_up, w_down):
    a, b = jnp.split(u @ w_up, 2, axis=-1)
    return (jax.nn.silu(a) * b) @ w_down


def sweep_queries(fn, n_q, block, *qs):
    c = n_q if n_q <= block else math.gcd(n_q, block)
    n = n_q // c
    if n == 1:
        return fn(0, *qs)
    blocks = tuple(jnp.moveaxis(q.reshape((q.shape[0], n, c) + q.shape[2:]), 1, 0) for q in qs)
    out = lax.map(lambda a: fn(a[0] * c, *a[1:]), (jnp.arange(n),) + blocks)
    out = jnp.moveaxis(out, 0, 1)
    return out.reshape((out.shape[0], n * c) + out.shape[3:])


def diff_attention(q, k, v, q_pos0, slopes, lam):
    k_pos = jnp.arange(k.shape[1])
    scale = HEAD_DIM ** -0.5

    def block(t0, qb):
        t_pos = q_pos0 + t0 + jnp.arange(qb.shape[1])
        dist = t_pos[:, None] - k_pos[None, :]
        s = jnp.einsum('bthmd,bshmd->bhmts', qb, k).astype(jnp.float32) * scale
        s = s - slopes[None, :, None, None, None] * dist.astype(jnp.float32)
        s = jnp.where(dist >= 0, s, -jnp.inf)
        p = jax.nn.softmax(s, axis=-1)
        o = jnp.einsum('bhmts,bshe->bthme', p.astype(v.dtype), v)
        return o[:, :, :, 0] - lam * o[:, :, :, 1]

    return sweep_queries(block, q.shape[1], ATTN_Q_BLOCK, q)


def forgetting_attention(q, k, v, cum_q, cum_k, q_pos0):
    k_pos = jnp.arange(k.shape[1])
    cum_k_h = jnp.swapaxes(cum_k, 1, 2)
    scale = HEAD_DIM ** -0.5

    def block(t0, qb, cqb):
        t_pos = q_pos0 + t0 + jnp.arange(qb.shape[1])
        causal = t_pos[:, None] >= k_pos[None, :]
        decay = jnp.swapaxes(cqb, 1, 2)[:, :, :, None] - cum_k_h[:, :, None, :]
        s = jnp.einsum('bthd,bshd->bhts', qb, k).astype(jnp.float32) * scale + decay
        s = jnp.where(causal, s, -jnp.inf)
        p = jax.nn.softmax(s, axis=-1)
        return jnp.einsum('bhts,bshd->bthd', p.astype(v.dtype), v)

    return sweep_queries(block, q.shape[1], ATTN_Q_BLOCK, q, cum_q)


def moba_attention(q, k, v, q_pos0, slopes):
    bsz, s_len = k.shape[:2]
    nb = -(-s_len // MOBA_BLOCK)
    pad = nb * MOBA_BLOCK - s_len

    def to_blocks(a):
        a = jnp.pad(a, ((0, 0), (0, pad), (0, 0), (0, 0)))
        return a.reshape(bsz, nb, MOBA_BLOCK, HB, HEAD_DIM).transpose(0, 3, 1, 2, 4)

    kb, vb = to_blocks(k), to_blocks(v)
    k_mean = jnp.mean(kb.astype(jnp.float32), axis=3)
    blk_ids = jnp.arange(nb)
    k_sel = min(MOBA_TOPK, nb)
    scale = HEAD_DIM ** -0.5
    gather = jax.vmap(jax.vmap(lambda blocks, ix: blocks[ix]))

    def block(t0, qb):
        c = qb.shape[1]
        t_pos = q_pos0 + t0 + jnp.arange(c)
        own = t_pos // MOBA_BLOCK
        score = jnp.einsum('bthd,bhnd->bhtn', qb.astype(jnp.float32), k_mean)
        score = jnp.where(blk_ids[None, :] < own[:, None], score, -jnp.inf)
        _, top = lax.top_k(score, k_sel)
        valid = jnp.arange(k_sel)[None, :] < own[:, None]
        own_b = jnp.broadcast_to(own[None, None, :, None], top.shape[:3] + (1,)).astype(top.dtype)
        sel = jnp.concatenate([top, own_b], axis=-1)
        ok_sel = jnp.concatenate([valid, jnp.ones((c, 1), bool)], axis=-1)
        kg = gather(kb, sel)
        vg = gather(vb, sel)
        k_pos = sel[..., None] * MOBA_BLOCK + jnp.arange(MOBA_BLOCK)
        dist = t_pos[None, None, :, None, None] - k_pos
        s = jnp.einsum('bthd,bhtnjd->bhtnj', qb, kg).astype(jnp.float32) * scale
        s = s - slopes[None, :, None, None, None] * dist.astype(jnp.float32)
        s = jnp.where(ok_sel[None, None, :, :, None] & (dist >= 0), s, -jnp.inf)
        p = jax.nn.softmax(s.reshape(s.shape[:3] + (-1,)), axis=-1).reshape(s.shape)
        return jnp.einsum('bhtnj,bhtnjd->bthd', p.astype(vg.dtype), vg)

    return sweep_queries(block, q.shape[1], MOBA_Q_CHUNK, q)


def mixer_sublayer(u, l, past, w_in_l, b_forget_l, lambda_l, subln_l, w_branch_l, w_out_l, slopes):
    bsz, t_new, _ = u.shape
    z = u @ w_in_l
    q_a = z[..., :A_QK].reshape(bsz, t_new, HA, 2, HEAD_DIM)
    q_b = z[..., A_QK:A_QK + B_W].reshape(bsz, t_new, HB, HEAD_DIM)
    q_c = z[..., A_QK + B_W:Q_COLS].reshape(bsz, t_new, HC, HEAD_DIM)
    kv_new = z[..., Q_COLS:Q_COLS + KV_WIDTH]
    logf_new = jax.nn.log_sigmoid((z[..., F_COL:F_COL + HC] + b_forget_l).astype(jnp.float32))
    gates = jax.nn.sigmoid(z[..., G_COL:].reshape(bsz, t_new, N_BRANCH, D_MODEL))

    if past is None:
        segs = tuple(kv_new[..., lo:hi] for lo, hi in KV_SEGS)
        logf_all = logf_new
    else:
        past_segs, past_logf = past
        segs = tuple(jnp.concatenate([p, kv_new[..., lo:hi].astype(p.dtype)], axis=1)
                     for p, (lo, hi) in zip(past_segs, KV_SEGS))
        logf_all = jnp.concatenate([past_logf.astype(jnp.float32), logf_new], axis=1)
    k_a = segs[0].reshape(bsz, -1, HA, 2, HEAD_DIM)
    v_a = segs[1].reshape(bsz, -1, HA, 2 * HEAD_DIM)
    k_b = segs[2].reshape(bsz, -1, HB, HEAD_DIM)
    v_b = segs[3].reshape(bsz, -1, HB, HEAD_DIM)
    k_c = segs[4].reshape(bsz, -1, HC, HEAD_DIM)
    v_c = segs[5].reshape(bsz, -1, HC, HEAD_DIM)
    q_pos0 = k_a.shape[1] - t_new
    cum = jnp.cumsum(logf_all, axis=1)

    lam_init = 0.8 - 0.6 * math.exp(-0.3 * l)
    lf = lambda_l.astype(jnp.float32)
    lam = jnp.exp(jnp.dot(lf[0], lf[1])) - jnp.exp(jnp.dot(lf[2], lf[3])) + lam_init
    o_a = diff_attention(q_a, k_a, v_a, q_pos0, slopes[:HA], lam.astype(u.dtype))
    o_af = o_a.astype(jnp.float32)
    o_a = (o_af * lax.rsqrt(jnp.mean(jnp.square(o_af), -1, keepdims=True) + LN_EPS)
           * subln_l * (1.0 - lam_init)).astype(u.dtype)
    o_b = moba_attention(q_b, k_b, v_b, q_pos0, slopes[HA:])
    o_c = forgetting_attention(q_c, k_c, v_c, cum[:, q_pos0:], cum, q_pos0)

    br_a = o_a.reshape(bsz, t_new, A_V) @ w_branch_l[:A_V]
    br_b = o_b.reshape(bsz, t_new, B_W) @ w_branch_l[A_V:A_V + B_W]
    br_c = o_c.reshape(bsz, t_new, C_W) @ w_branch_l[A_V + B_W:]
    merged = gates[:, :, 0] * br_a + gates[:, :, 1] * br_b + gates[:, :, 2] * br_c
    return merged @ w_out_l, kv_new, logf_new


def moe_ffn(u, w_router, b_router, w_e_in, w_e_out):
    logits = (u @ w_router).astype(jnp.float32) + b_router
    probs = jax.nn.softmax(logits, axis=-1)
    top_p, top_i = lax.top_k(probs, TOP_K)
    top_p = top_p / jnp.sum(top_p, -1, keepdims=True)
    out = jnp.zeros_like(u)
    for e in range(N_EXPERTS):
        w_e = jnp.sum(jnp.where(top_i == e, top_p, 0.0), axis=-1)
        out = out + w_e[..., None].astype(u.dtype) * swiglu(u, w_e_in[e], w_e_out[e])
    return out


def run_trunk(x, c, past_fn, w_in, b_forget, lambda_qk, subln_gain, w_branch, w_out, w_ada, b_ada,
              ln_gain, ln_bias, w_ff_in, w_ff_out, w_router, b_router, w_exp_in, w_exp_out):
    slopes = alibi_slopes()
    kv_rows, logf_rows = [], []
    for l in range(DEPTH):
        shift, scale, gate = ada_params(c, w_ada[l, 0], b_ada[l, 0])
        u = x * (1.0 + scale) + shift
        h, kv_new, logf_new = mixer_sublayer(u, l, past_fn(l), w_in[l], b_forget[l], lambda_qk[l],
                                             subln_gain[l], w_branch[l], w_out[l], slopes)
        x = layer_norm(DEEPNORM_ALPHA * x + gate * h, ln_gain[l, 0], ln_bias[l, 0])
        kv_rows.append(kv_new)
        logf_rows.append(logf_new)

        shift, scale, gate = ada_params(c, w_ada[l, 1], b_ada[l, 1])
        u = x * (1.0 + scale) + shift
        if l % 2 == 0:
            h = swiglu(u, w_ff_in[l // 2], w_ff_out[l // 2])
        else:
            h = moe_ffn(u, w_router[l // 2], b_router[l // 2], w_exp_in[l // 2], w_exp_out[l // 2])
        x = layer_norm(DEEPNORM_ALPHA * x + gate * h, ln_gain[l, 1], ln_bias[l, 1])
    return x, jnp.stack(kv_rows, axis=1), jnp.stack(logf_rows, axis=1)


def gather_past(cache, page_table, l, lo, hi):
    rows = cache[page_table, l, :, lo:hi]
    return rows.reshape(rows.shape[0], -1, hi - lo)


def setup_inputs(seed: int = 0) -> dict:
    key = jax.random.key(seed)
    ks = jax.random.split(key, 24)
    n_pages = PAST_LEN // PAGE_SIZE
    n_pool = (DEC_BATCH * n_pages * 5) // 4

    def nrm(k, shape, s):
        return jax.random.normal(k, shape, jnp.float32) * s

    x_prompt = nrm(ks[0], (BATCH, SEQ, D_MODEL), 1.0)
    x_sample = nrm(ks[1], (DEC_BATCH, DEC_SEQ, D_MODEL), 1.0)
    cache_kv = nrm(ks[2], (n_pool, DEPTH, PAGE_SIZE, KV_WIDTH), 1.0)
    cache_logf = jax.nn.log_sigmoid(FORGET_BIAS + nrm(ks[3], (n_pool, DEPTH, PAGE_SIZE, HC), 1.0))
    page_table = jax.random.permutation(ks[4], n_pool)[:DEC_BATCH * n_pages].reshape(
        DEC_BATCH, n_pages).astype(jnp.int32)
    c_prompt = nrm(ks[5], (BATCH, D_MODEL), 1.0)
    c_sample = nrm(ks[6], (DEC_BATCH, D_MODEL), 1.0)

    w_in = nrm(ks[7], (DEPTH, D_MODEL, IN_COLS), D_MODEL ** -0.5)
    b_forget = FORGET_BIAS + nrm(ks[8], (DEPTH, HC), 0.1)
    lambda_qk = nrm(ks[9], (DEPTH, 4, HEAD_DIM), 0.1)
    subln_gain = 1.0 + nrm(ks[10], (DEPTH, 2 * HEAD_DIM), 0.01)
    branch_scale = jnp.concatenate([jnp.full((A_V,), A_V ** -0.5), jnp.full((B_W,), B_W ** -0.5),
                                    jnp.full((C_W,), C_W ** -0.5)]).astype(jnp.float32)
    w_branch = nrm(ks[11], (DEPTH, MIX_WIDTH, D_MODEL), 1.0) * branch_scale[:, None]
    w_out = nrm(ks[12], (DEPTH, D_MODEL, D_MODEL), D_MODEL ** -0.5 * DEEPNORM_BETA)
    w_ada = nrm(ks[13], (DEPTH, 2, D_MODEL, 3 * D_MODEL), 0.5 * D_MODEL ** -0.5)
    b_ada = nrm(ks[14], (DEPTH, 2, 3 * D_MODEL), 0.01)
    ln_gain = 1.0 + nrm(ks[15], (DEPTH, 2, D_MODEL), 0.01)
    ln_bias = nrm(ks[16], (DEPTH, 2, D_MODEL), 0.01)
    w_ff_in = nrm(ks[17], (N_DENSE, D_MODEL, 2 * D_FF), D_MODEL ** -0.5)
    w_ff_out = nrm(ks[18], (N_DENSE, D_FF, D_MODEL), D_FF ** -0.5 * DEEPNORM_BETA)
    w_router = nrm(ks[19], (N_MOE, D_MODEL, N_EXPERTS), D_MODEL ** -0.5)
    b_router = nrm(ks[20], (N_MOE, N_EXPERTS), 0.01)
    w_exp_in = nrm(ks[21], (N_MOE, N_EXPERTS, D_MODEL, 2 * E_FF), D_MODEL ** -0.5)
    w_exp_out = nrm(ks[22], (N_MOE, N_EXPERTS, E_FF, D_MODEL), E_FF ** -0.5 * DEEPNORM_BETA)
    return {'x_prompt': x_prompt, 'x_sample': x_sample, 'cache_kv': cache_kv, 'cache_logf': cache_logf,
            'page_table': page_table, 'c_prompt': c_prompt, 'c_sample': c_sample,
            'w_in': w_in, 'b_forget': b_forget, 'lambda_qk': lambda_qk, 'subln_gain': subln_gain,
            'w_branch': w_branch, 'w_out': w_out, 'w_ada': w_ada, 'b_ada': b_ada,
            'ln_gain': ln_gain, 'ln_bias': ln_bias, 'w_ff_in': w_ff_in, 'w_ff_out': w_ff_out,
            'w_router': w_router, 'b_router': b_router, 'w_exp_in': w_exp_in, 'w_exp_out': w_exp_out}


def reference(x_prompt, x_sample, cache_kv, cache_logf, page_table, c_prompt, c_sample,
              w_in, b_forget, lambda_qk, subln_gain, w_branch, w_out, w_ada, b_ada,
              ln_gain, ln_bias, w_ff_in, w_ff_out, w_router, b_router, w_exp_in, w_exp_out):
    def prompt_past(l):
        return None

    def sample_past(l):
        segs = tuple(gather_past(cache_kv, page_table, l, lo, hi) for lo, hi in KV_SEGS)
        logf = cache_logf[page_table, l]
        return segs, logf.reshape(logf.shape[0], -1, HC)

    y_prompt, new_kv_prompt, new_logf_prompt = run_trunk(
        x_prompt, c_prompt, prompt_past, w_in, b_forget, lambda_qk, subln_gain, w_branch, w_out,
        w_ada, b_ada, ln_gain, ln_bias, w_ff_in, w_ff_out, w_router, b_router, w_exp_in, w_exp_out)
    y_sample, new_kv_sample, new_logf_sample = run_trunk(
        x_sample, c_sample, sample_past, w_in, b_forget, lambda_qk, subln_gain, w_branch, w_out,
        w_ada, b_ada, ln_gain, ln_bias, w_ff_in, w_ff_out, w_router, b_router, w_exp_in, w_exp_out)
    return (y_prompt, y_sample, new_kv_prompt, new_logf_prompt, new_kv_sample, new_logf_sample)
```

```python
import functools
import math

import jax
import jax.numpy as jnp
import numpy as np
from jax import lax
from jax.experimental import pallas as pl
from jax.experimental.pallas import tpu as pltpu

D_MODEL = 1024
HEAD_DIM = 64
HA, HB, HC = 4, 4, 4
A_QK = HA * 2 * HEAD_DIM
A_V = HA * 2 * HEAD_DIM
B_W = HB * HEAD_DIM
C_W = HC * HEAD_DIM
N_BRANCH = 3
Q_COLS = A_QK + B_W + C_W
KV_WIDTH = A_QK + A_V + 2 * B_W + 2 * C_W
F_COL = Q_COLS + KV_WIDTH
G_COL = F_COL + HC
MOBA_BLOCK = 256
MOBA_TOPK = 3
N_EXPERTS = 8
TOP_K = 2
LN_EPS = 1e-5
PAGE = 128

KA0, VA0, KB0, VB0, KC0, VC0 = 0, 512, 1024, 1280, 1536, 1792

LANES = 128
HALF = LANES // 2
VMEM_LIMIT = 56 * 1024 * 1024

N_UNITS = 8
MASK_NEG = -(2.0 ** 100)

F32 = jnp.float32
BF16 = jnp.bfloat16


def _alibi_slopes():
    n = HA + HB
    s = 2.0 ** (-8.0 * (np.arange(n, dtype=np.float64) + 1.0) / n)
    mant = np.frexp(s)[0] * 256.0
    assert np.all(mant == np.round(mant)), "slopes must be exact in bfloat16"
    return s


SLOPES = _alibi_slopes()


def _cparams(sem):
    return pltpu.CompilerParams(dimension_semantics=sem, vmem_limit_bytes=VMEM_LIMIT)


def _sigmoid(x):
    return 1.0 / (1.0 + jnp.exp(-x))


def _silu(x):
    return x * _sigmoid(x)


def _layer_norm(x, g, b):
    mu = jnp.mean(x, axis=-1, keepdims=True)
    xc = x - mu
    var = jnp.mean(xc * xc, axis=-1, keepdims=True)
    return xc * lax.rsqrt(var + LN_EPS) * g + b


def _dot(a, b):
    return jnp.dot(a, b, preferred_element_type=F32)


def _dot_nt(a, b, precision=None):
    return lax.dot_general(a, b, (((1,), (1,)), ((), ())), preferred_element_type=F32, precision=precision)


def _ada_kernel(c_ref, w_ref, b_ref, o_ref):
    c = _silu(c_ref[...]).astype(BF16)
    o_ref[0] = _dot(c, w_ref[0].astype(BF16)) + b_ref[0]


def _ada(c_all, w_ada4, b_ada4):
    n, d = c_all.shape
    k, _, n3 = w_ada4.shape
    tn = 1024
    return pl.pallas_call(
        _ada_kernel,
        grid=(k, n3 // tn),
        in_specs=[pl.BlockSpec((n, d), lambda i, j: (0, 0)),
                  pl.BlockSpec((1, d, tn), lambda i, j: (i, 0, j)),
                  pl.BlockSpec((1, 1, tn), lambda i, j: (i, 0, j))],
        out_specs=pl.BlockSpec((1, n, tn), lambda i, j: (i, 0, j)),
        out_shape=jax.ShapeDtypeStruct((k, n, n3), F32),
        compiler_params=_cparams(("arbitrary", "arbitrary")),
        name="ada",
    )(c_all, w_ada4, b_ada4.reshape(k, 1, n3))


def _in_proj_kernel(x_ref, shift_ref, scale_ref, w_ref, wf_ref, bf_ref,
                    kv_ref, logf_ref, qbf_ref, kvbf_ref, qb_ref):
    u = (x_ref[0] * (1.0 + scale_ref[0]) + shift_ref[0]).astype(BF16)
    zq = _dot(u, w_ref[:, 0:Q_COLS]) * (HEAD_DIM ** -0.5)
    qbf_ref[0] = zq.astype(BF16)
    qb_ref[0] = zq[:, A_QK:A_QK + B_W]
    for c in range(KV_WIDTH // 1024):
        lo = c * 1024
        zkv = _dot(u, w_ref[:, Q_COLS + lo:Q_COLS + lo + 1024])
        kv_ref[0, :, lo:lo + 1024] = zkv
        kvbf_ref[0, :, lo:lo + 1024] = zkv.astype(BF16)
    zf = _dot(u, wf_ref[...]) + bf_ref[...]
    ls = jnp.minimum(zf, 0.0) - jnp.log(1.0 + jnp.exp(-jnp.abs(zf)))
    logf_ref[0] = ls[:, 0:HC]


def _in_proj(x3, mod3, w_qkv, w_f, b_f, tm):
    s, t, d = x3.shape
    tmod = mod3.shape[1]
    rows = (lambda i, j: (i, j, 0))
    if tmod == 1:
        mod_map = lambda k: (lambda i, j: (i, 0, k))
    else:
        mod_map = lambda k: (lambda i, j: (i, j, k))
    bm = 1 if tmod == 1 else tm
    return pl.pallas_call(
        _in_proj_kernel,
        grid=(s, t // tm),
        in_specs=[pl.BlockSpec((1, tm, d), rows),
                  pl.BlockSpec((1, bm, d), mod_map(0)),
                  pl.BlockSpec((1, bm, d), mod_map(1)),
                  pl.BlockSpec(w_qkv.shape, lambda i, j: (0, 0)),
                  pl.BlockSpec(w_f.shape, lambda i, j: (0, 0)),
                  pl.BlockSpec(b_f.shape, lambda i, j: (0, 0))],
        out_specs=[pl.BlockSpec((1, tm, KV_WIDTH), rows),
                   pl.BlockSpec((1, tm, HC), rows),
                   pl.BlockSpec((1, tm, Q_COLS), rows),
                   pl.BlockSpec((1, tm, KV_WIDTH), rows),
                   pl.BlockSpec((1, tm, B_W), rows)],
        out_shape=[jax.ShapeDtypeStruct((s, t, KV_WIDTH), F32),
                   jax.ShapeDtypeStruct((s, t, HC), F32),
                   jax.ShapeDtypeStruct((s, t, Q_COLS), BF16),
                   jax.ShapeDtypeStruct((s, t, KV_WIDTH), BF16),
                   jax.ShapeDtypeStruct((s, t, B_W), F32)],
        compiler_params=_cparams(("arbitrary", "arbitrary")),
        name="in_proj",
    )(x3, mod3, mod3, w_qkv, w_f, b_f)


def _prep_kernel(n_blocks, logf_ref, kb_ref, qb_ref, qx_ref, kx_ref, carry_ref, km_ref):
    t = pl.program_id(1)
    tp = MOBA_BLOCK
    lane = lax.broadcasted_iota(jnp.int32, (tp, LANES), 1)
    sub = lane % HALF
    row = lax.broadcasted_iota(jnp.int32, (tp, LANES), 0)

    @pl.when(t == 0)
    def _():
        carry_ref[...] = jnp.zeros_like(carry_ref)
        km_ref[...] = jnp.zeros_like(km_ref)

    pos = t * tp + row
    kx_pos = jnp.where(sub == 0, (pos // HALF).astype(F32),
                       jnp.where(sub == 1, (pos % HALF).astype(F32), 0.0))
    for h in range(HA):
        sl = float(SLOPES[h])
        qx_ref[0, h] = jnp.where(sub == 0, sl * HALF, jnp.where(sub == 1, sl, 0.0)).astype(BF16)
        kx_ref[0, h] = kx_pos.astype(BF16)

    lf = logf_ref[0]
    r2 = lax.broadcasted_iota(jnp.int32, (tp, tp), 0)
    c2 = lax.broadcasted_iota(jnp.int32, (tp, tp), 1)
    tri = (c2 <= r2).astype(F32)
    cum = jnp.dot(tri, lf, preferred_element_type=F32, precision=lax.Precision.HIGHEST)
    qx_c = jnp.where(sub < 3, -1.0, 0.0).astype(BF16)
    for j in range(HC // 2):
        kx = jnp.zeros((tp, LANES), F32)
        for e in range(2):
            h = 2 * j + e
            c = cum[:, h:h + 1] + carry_ref[h:h + 1, 0:1]
            carry_ref[h:h + 1, :] = jnp.broadcast_to(c[tp - 1:tp, :], (1, LANES))
            hi = c.astype(BF16).astype(F32)
            r1 = c - hi
            mid = r1.astype(BF16).astype(F32)
            lo = (r1 - mid).astype(BF16).astype(F32)
            base = e * HALF
            kx = jnp.where(lane == base, hi, jnp.where(lane == base + 1, mid,
                                                       jnp.where(lane == base + 2, lo, kx)))
        u = HA + HB // 2 + j
        kx_ref[0, u] = kx.astype(BF16)
        qx_ref[0, u] = qx_c

    kb = kb_ref[0]
    qb = qb_ref[0]
    lane_b = lax.broadcasted_iota(jnp.int32, (tp, B_W), 1)
    nblk = sub - 2
    in_win = (nblk >= 0) & (nblk < HALF - 2)
    past = in_win & (nblk < t)
    for j in range(HB // 2):
        sc = None
        for e in range(2):
            h = 2 * j + e
            qm = jnp.where((lane_b // HEAD_DIM) == h, qb, 0.0)
            s_h = _dot_nt(qm, km_ref[...], precision=lax.Precision.HIGHEST)
            sc = s_h if e == 0 else jnp.where(lane < HALF, sc, s_h)
        sc = jnp.where(past, sc, -jnp.inf)
        rank = jnp.zeros((tp, LANES), jnp.int32)
        for d in range(1, n_blocks):
            lower = pltpu.roll(sc, d, 1)
            upper = pltpu.roll(sc, LANES - d, 1)
            rank = rank + (lower >= sc).astype(jnp.int32) + (upper > sc).astype(jnp.int32)
        keep = past & (rank < MOBA_TOPK)
        sel = jnp.where(past & jnp.logical_not(keep), MASK_NEG, 0.0)
        u = HA + j
        qx = sel
        kx = jnp.where(in_win & (nblk == t), 1.0, kx_pos)
        for e in range(2):
            sl = float(SLOPES[HA + 2 * j + e])
            base = e * HALF
            qx = jnp.where(lane == base, sl * HALF, jnp.where(lane == base + 1, sl, qx))
        qx_ref[0, u] = qx.astype(BF16)
        kx_ref[0, u] = kx.astype(BF16)

    kmean = jnp.sum(kb, axis=0, keepdims=True) * (1.0 / tp)
    km_ref[pl.ds(t + 2, 1), :] = kmean
    km_ref[pl.ds(t + 2 + HALF, 1), :] = kmean


def _prep(logf, kv, qb):
    b, t, _ = logf.shape
    tp = MOBA_BLOCK
    nb = t // tp
    assert t % tp == 0 and nb <= HALF // 2
    rows = lambda i, j: (i, j, 0)
    xs = pl.BlockSpec((1, N_UNITS, tp, LANES), lambda i, j: (i, 0, j, 0))
    return pl.pallas_call(
        functools.partial(_prep_kernel, nb),
        grid=(b, nb),
        in_specs=[pl.BlockSpec((1, tp, HC), rows),
                  pl.BlockSpec((1, tp, B_W), lambda i, j: (i, j, KB0 // B_W)),
                  pl.BlockSpec((1, tp, B_W), rows)],
        out_specs=[xs, xs],
        out_shape=[jax.ShapeDtypeStruct((b, N_UNITS, t, LANES), BF16)] * 2,
        scratch_shapes=[pltpu.VMEM((8, LANES), F32), pltpu.VMEM((LANES, B_W), F32)],
        compiler_params=_cparams(("arbitrary", "arbitrary")),
        name="prep",
    )(logf, kv, qb)


def _lambda(lam_ref, lam_init):
    lq = lam_ref[...]
    d01 = jnp.sum(lq[0:1] * lq[1:2], axis=1, keepdims=True)
    d23 = jnp.sum(lq[2:3] * lq[3:4], axis=1, keepdims=True)
    return jnp.exp(d01) - jnp.exp(d23) + lam_init


def _attn_kernel(mode, tq, lam_init, *refs):
    if mode == "diff":
        q_ref, qx_ref, k_ref, kx_ref, v_ref, lam_ref, subln_ref, o_ref, qs_ref, m_ref, l_ref, acc_ref = refs
    else:
        q_ref, qx_ref, k_ref, kx_ref, v_ref, o_ref, qs_ref, m_ref, l_ref, acc_ref = refs
    t = q_ref.shape[1]
    nq = t // tq
    lane2 = lax.broadcasted_iota(jnp.int32, (tq, 2 * LANES), 1)
    first = (lane2 % LANES) < HALF
    row_s = lax.broadcasted_iota(jnp.int32, (2 * tq, tq), 0) % tq
    col_s = lax.broadcasted_iota(jnp.int32, (2 * tq, tq), 1)
    causal = col_s <= row_s

    def kv_step(j, masked):
        c0 = pl.multiple_of(j * tq, tq)
        kf = jnp.concatenate([k_ref[0, pl.ds(c0, tq), :], kx_ref[0, 0, pl.ds(c0, tq), :]], axis=1)
        s = _dot_nt(qs_ref[...], kf)
        if masked:
            s = jnp.where(causal, s, -jnp.inf)
        m_prev = m_ref[...]
        m_new = jnp.maximum(m_prev, jnp.max(s, axis=1, keepdims=True))
        alpha = jnp.exp(m_prev - m_new)
        p = jnp.exp(s - m_new)
        l_ref[...] = alpha * l_ref[...] + jnp.sum(p, axis=1, keepdims=True)
        acc_ref[...] = alpha * acc_ref[...] + _dot(p.astype(BF16), v_ref[0, pl.ds(c0, tq), :])
        m_ref[...] = m_new

    def q_step(i, carry):
        r0 = pl.multiple_of(i * tq, tq)
        qf = jnp.concatenate([q_ref[0, pl.ds(r0, tq), :], qx_ref[0, 0, pl.ds(r0, tq), :]], axis=1)
        zero = jnp.zeros_like(qf)
        qs_ref[0:tq, :] = jnp.where(first, qf, zero)
        qs_ref[tq:2 * tq, :] = jnp.where(first, zero, qf)
        m_ref[...] = jnp.full_like(m_ref, -jnp.inf)
        l_ref[...] = jnp.zeros_like(l_ref)
        acc_ref[...] = jnp.zeros_like(acc_ref)
        lax.fori_loop(0, i, lambda j, c: (kv_step(j, False), c)[1], 0)
        kv_step(i, True)
        o = acc_ref[...] / l_ref[...]
        o1, o2 = o[0:tq], o[tq:2 * tq]
        if mode == "diff":
            od = o1 - _lambda(lam_ref, lam_init) * o2
            od = od * lax.rsqrt(jnp.mean(od * od, axis=-1, keepdims=True) + LN_EPS)
            res = od * subln_ref[...] * (1.0 - lam_init)
        else:
            lane = lax.broadcasted_iota(jnp.int32, (tq, LANES), 1)
            res = jnp.where(lane < HALF, o1, o2)
        o_ref[0, pl.ds(r0, tq), :] = res.astype(o_ref.dtype)
        return carry

    lax.fori_loop(0, nq, q_step, 0)


def _attn(mode, q_bf, qx, kv_bf, kx, lam, subln, lam_init, tq=256):
    b, t, _ = q_bf.shape
    if mode == "diff":
        n_u, u0, q0, k0, v0 = HA, 0, 0, KA0 // LANES, VA0 // LANES
    elif mode == "moba":
        n_u, u0, q0, k0, v0 = HB // 2, HA, A_QK // LANES, KB0 // LANES, VB0 // LANES
    else:
        n_u, u0, q0, k0, v0 = HC // 2, HA + HB // 2, (A_QK + B_W) // LANES, KC0 // LANES, VC0 // LANES
    col = lambda off: pl.BlockSpec((1, t, LANES), lambda i, u: (i, 0, off + u))
    ext = pl.BlockSpec((1, 1, t, LANES), lambda i, u: (i, u0 + u, 0, 0))
    in_specs = [col(q0), ext, col(k0), ext, col(v0)]
    args = [q_bf, qx, kv_bf, kx, kv_bf]
    if mode == "diff":
        in_specs += [pl.BlockSpec(lam.shape, lambda i, u: (0, 0)),
                     pl.BlockSpec(subln.shape, lambda i, u: (0, 0))]
        args += [lam, subln]
    return pl.pallas_call(
        functools.partial(_attn_kernel, mode, tq, lam_init),
        grid=(b, n_u),
        in_specs=in_specs,
        out_specs=pl.BlockSpec((1, t, LANES), lambda i, u: (i, 0, u)),
        out_shape=jax.ShapeDtypeStruct((b, t, n_u * LANES), BF16),
        scratch_shapes=[pltpu.VMEM((2 * tq, 2 * LANES), BF16),
                        pltpu.VMEM((2 * tq, 1), F32),
                        pltpu.VMEM((2 * tq, 1), F32),
                        pltpu.VMEM((2 * tq, LANES), F32)],
        compiler_params=_cparams(("arbitrary", "arbitrary")),
        name="attn_" + mode,
    )(*args)


N_ROWS = 2 * HA + HB + HC
ROW_B = 2 * HA
ROW_C = 2 * HA + HB
VCAT = A_V + B_W + C_W


def _decode_kernel(g_pages, n_pages, lam_init, pt_ref, *refs):
    kv_refs = refs[:g_pages]
    lf_refs = refs[g_pages:2 * g_pages]
    (qbd_ref, qrows_ref, kvnew_ref, lfnew_ref, qb_ref, slane_ref, srows_ref, lam_ref, subln_ref,
     oa_ref, ob_ref, oc_ref,
     m_ref, l_ref, acc_ref, carry_ref, mb_ref, lb_ref, ob_scr, km_ref) = refs[2 * g_pages:]
    g = pl.program_id(1)
    n_steps = n_pages // g_pages
    blk_pages = MOBA_BLOCK // PAGE
    blocks_per_step = g_pages // blk_pages
    n_blocks = n_pages // blk_pages

    @pl.when(g == 0)
    def _():
        m_ref[...] = jnp.full_like(m_ref, -jnp.inf)
        l_ref[...] = jnp.zeros_like(l_ref)
        acc_ref[...] = jnp.zeros_like(acc_ref)
        carry_ref[...] = jnp.zeros_like(carry_ref)

    lane = lax.broadcasted_iota(jnp.int32, (PAGE, LANES), 1)
    row = lax.broadcasted_iota(jnp.int32, (PAGE, LANES), 0)
    tri = (lane <= row).astype(F32)
    slane = slane_ref[...]

    for blk in range(blocks_per_step):
        s_parts, v_parts, kb_sum = [], [], None
        for pi in range(blk_pages):
            i = blk * blk_pages + pi
            pg_ref = kv_refs[i]
            kcat = jnp.concatenate([pg_ref[0, 0, :, KA0:KA0 + A_QK].astype(BF16),
                                    pg_ref[0, 0, :, KB0:KB0 + B_W].astype(BF16),
                                    pg_ref[0, 0, :, KC0:KC0 + C_W].astype(BF16)], axis=1)
            st = _dot(kcat, qbd_ref[0])
            pos = ((g * g_pages + i) * PAGE + row).astype(F32)
            st = st + slane * pos
            lf = lf_refs[i][0, 0]
            cum = jnp.dot(tri, lf, preferred_element_type=F32, precision=lax.Precision.HIGHEST)
            for h in range(HC):
                c = cum[:, h:h + 1] + carry_ref[h:h + 1, 0:1]
                carry_ref[h:h + 1, :] = jnp.broadcast_to(c[PAGE - 1:PAGE, :], (1, LANES))
                st = jnp.where(lane == ROW_C + h, st - c, st)
            s_parts.append(st.T[0:N_ROWS])
            v_parts.append(jnp.concatenate([pg_ref[0, 0, :, VA0:VA0 + A_V].astype(BF16),
                                            pg_ref[0, 0, :, VB0:VB0 + B_W].astype(BF16),
                                            pg_ref[0, 0, :, VC0:VC0 + C_W].astype(BF16)], axis=1))
            ks = jnp.sum(pg_ref[0, 0, :, KB0:KB0 + B_W], axis=0, keepdims=True)
            kb_sum = ks if kb_sum is None else kb_sum + ks
        s = jnp.concatenate(s_parts, axis=1)
        v = jnp.concatenate(v_parts, axis=0)
        m_blk = jnp.max(s, axis=1, keepdims=True)
        p = jnp.exp(s - m_blk)
        l_blk = jnp.sum(p, axis=1, keepdims=True)
        o_blk = _dot(p.astype(BF16), v)
        m_prev = m_ref[...]
        m_new = jnp.maximum(m_prev, m_blk)
        a_prev = jnp.exp(m_prev - m_new)
        a_blk = jnp.exp(m_blk - m_new)
        l_ref[...] = a_prev * l_ref[...] + a_blk * l_blk
        acc_ref[...] = a_prev * acc_ref[...] + a_blk * o_blk
        m_ref[...] = m_new
        n = g * blocks_per_step + blk
        for h in range(HB):
            r = ROW_B + h
            mb_ref[h, pl.ds(n, 1), :] = jnp.broadcast_to(m_blk[r:r + 1, :], (1, LANES))
            lb_ref[h, pl.ds(n, 1), :] = jnp.broadcast_to(l_blk[r:r + 1, :], (1, LANES))
            ob_scr[h, pl.ds(n, 1), :] = o_blk[r:r + 1, A_V:A_V + B_W]
        km_ref[pl.ds(n, 1), :] = kb_sum * (1.0 / MOBA_BLOCK)

    @pl.when(g == n_steps - 1)
    def _():
        kvn = kvnew_ref[0]
        kn = jnp.concatenate([kvn[:, KA0:KA0 + A_QK], kvn[:, KB0:KB0 + B_W], kvn[:, KC0:KC0 + C_W]], axis=1)
        vn = jnp.concatenate([kvn[:, VA0:VA0 + A_V], kvn[:, VB0:VB0 + B_W], kvn[:, VC0:VC0 + C_W]], axis=1)
        s_self = jnp.sum(qrows_ref[0] * kn, axis=1, keepdims=True)
        s_self = s_self + srows_ref[:, 0:1] * float(n_pages * PAGE)
        lfn = lfnew_ref[0]
        m_all, l_all, acc_all = m_ref[...], l_ref[...], acc_ref[...]

        def finish(r, s_r, lo, width):
            m_r = m_all[r:r + 1, :]
            m_f = jnp.maximum(m_r, s_r)
            a = jnp.exp(m_r - m_f)
            e = jnp.exp(s_r - m_f)
            return (a * acc_all[r:r + 1, lo:lo + width] + e * vn[:, lo:lo + width]) / (a * l_all[r:r + 1, :] + e)

        lam = _lambda(lam_ref, lam_init)
        for h in range(HA):
            o1 = finish(2 * h, s_self[2 * h:2 * h + 1, :], h * LANES, LANES)
            o2 = finish(2 * h + 1, s_self[2 * h + 1:2 * h + 2, :], h * LANES, LANES)
            od = o1 - lam * o2
            od = od * lax.rsqrt(jnp.mean(od * od, axis=-1, keepdims=True) + LN_EPS)
            oa_ref[0, h:h + 1, :] = od * subln_ref[...] * (1.0 - lam_init)
        for h in range(HC):
            r = ROW_C + h
            s_r = s_self[r:r + 1, :] - (carry_ref[h:h + 1, 0:1] + lfn[:, h:h + 1])
            oc_ref[0, h:h + 1, :] = finish(r, s_r, A_V + B_W + h * HEAD_DIM, HEAD_DIM)
        rowb = lax.broadcasted_iota(jnp.int32, (n_blocks, LANES), 0)
        for h in range(HB):
            r = ROW_B + h
            seg = slice(h * HEAD_DIM, (h + 1) * HEAD_DIM)
            sc = jnp.sum(km_ref[:, seg] * qb_ref[0][:, seg], axis=1, keepdims=True)
            sc = jnp.broadcast_to(sc, (n_blocks, LANES))
            keep = jnp.zeros((n_blocks, LANES), jnp.bool_)
            for _ in range(min(MOBA_TOPK, n_blocks)):
                mx = jnp.max(sc, axis=0, keepdims=True)
                idx = jnp.min(jnp.where(sc == mx, rowb, n_blocks), axis=0, keepdims=True)
                hit = rowb == idx
                keep = keep | hit
                sc = jnp.where(hit, -jnp.inf, sc)
            s_r = s_self[r:r + 1, :]
            mb = jnp.where(keep, mb_ref[h], -jnp.inf)
            m_f = jnp.maximum(jnp.max(mb, axis=0, keepdims=True), s_r)
            w = jnp.where(keep, jnp.exp(mb - m_f), 0.0)
            e = jnp.exp(s_r - m_f)
            l_f = jnp.sum(w * lb_ref[h], axis=0, keepdims=True) + e
            o_f = jnp.sum(w[:, 0:HEAD_DIM] * ob_scr[h][:, seg], axis=0, keepdims=True)
            lo = A_V + h * HEAD_DIM
            ob_ref[0, h:h + 1, :] = (o_f + e[:, 0:HEAD_DIM] * vn[:, lo:lo + HEAD_DIM]) / l_f[:, 0:HEAD_DIM]


def _decode(layer, cache_kv, cache_logf, page_table, qbd, qrows, kvnew, lfnew, qb, lam, subln, lam_init,
            g_pages=8):
    nseq, n_pages = page_table.shape
    assert n_pages % g_pages == 0 and g_pages % (MOBA_BLOCK // PAGE) == 0
    n_blocks = n_pages * PAGE // MOBA_BLOCK
    slane = np.zeros((1, LANES), np.float32)
    srows = np.zeros((N_ROWS, LANES), np.float32)
    for r in range(2 * HA):
        slane[0, r] = SLOPES[r // 2]
        srows[r, :] = SLOPES[r // 2]
    for h in range(HB):
        slane[0, ROW_B + h] = SLOPES[HA + h]
        srows[ROW_B + h, :] = SLOPES[HA + h]

    def page_spec(i, width):
        return pl.BlockSpec((1, 1, PAGE, width), lambda b, g, pt: (pt[b, g * g_pages + i], layer, 0, 0))

    per_seq = lambda shape: pl.BlockSpec((1,) + shape, lambda b, g, pt: (b, 0, 0))
    const = lambda a: pl.BlockSpec(a.shape, lambda b, g, pt: (0, 0))
    in_specs = ([page_spec(i, KV_WIDTH) for i in range(g_pages)]
                + [page_spec(i, HC) for i in range(g_pages)]
                + [per_seq((Q_COLS, LANES)), per_seq((N_ROWS, Q_COLS)), per_seq((1, KV_WIDTH)),
                   per_seq((1, LANES)), per_seq((1, B_W)),
                   const(slane), const(srows), const(lam), const(subln)])
    grid_spec = pltpu.PrefetchScalarGridSpec(
        num_scalar_prefetch=1,
        grid=(nseq, n_pages // g_pages),
        in_specs=in_specs,
        out_specs=[per_seq((HA, LANES)), per_seq((HB, HEAD_DIM)), per_seq((HC, HEAD_DIM))],
        scratch_shapes=[pltpu.VMEM((N_ROWS, 1), F32), pltpu.VMEM((N_ROWS, 1), F32),
                        pltpu.VMEM((N_ROWS, VCAT), F32), pltpu.VMEM((8, LANES), F32),
                        pltpu.VMEM((HB, n_blocks, LANES), F32), pltpu.VMEM((HB, n_blocks, LANES), F32),
                        pltpu.VMEM((HB, n_blocks, B_W), F32), pltpu.VMEM((n_blocks, B_W), F32)])
    return pl.pallas_call(
        functools.partial(_decode_kernel, g_pages, n_pages, lam_init),
        grid_spec=grid_spec,
        out_shape=[jax.ShapeDtypeStruct((nseq, HA, LANES), F32),
                   jax.ShapeDtypeStruct((nseq, HB, HEAD_DIM), F32),
                   jax.ShapeDtypeStruct((nseq, HC, HEAD_DIM), F32)],
        compiler_params=_cparams(("arbitrary", "arbitrary")),
        name="decode",
    )(page_table, *([cache_kv] * g_pages), *([cache_logf] * g_pages),
      qbd, qrows, kvnew, lfnew, qb, jnp.asarray(slane), jnp.asarray(srows), lam, subln)


def _post_kernel(alpha, x_ref, shift_ref, scale_ref, gate_ref, oa_ref, ob_ref, oc_ref,
                 wg_ref, wb_ref, wo_ref, g_ref, b_ref, y_ref):
    x = x_ref[0]
    u = (x * (1.0 + scale_ref[0]) + shift_ref[0]).astype(BF16)
    d = D_MODEL
    merged = _sigmoid(_dot(u, wg_ref[:, 0:d])) * _dot(oa_ref[0].astype(BF16), wb_ref[0:A_V, :])
    merged += _sigmoid(_dot(u, wg_ref[:, d:2 * d])) * _dot(ob_ref[0].astype(BF16), wb_ref[A_V:A_V + B_W, :])
    merged += _sigmoid(_dot(u, wg_ref[:, 2 * d:3 * d])) * _dot(oc_ref[0].astype(BF16), wb_ref[A_V + B_W:, :])
    h = _dot(merged.astype(BF16), wo_ref[...])
    y_ref[0] = _layer_norm(alpha * x + gate_ref[0] * h, g_ref[...], b_ref[...])


def _mod_specs(tmod, tm, d, n_grid):
    bm = 1 if tmod == 1 else tm
    if n_grid == 2:
        if tmod == 1:
            return [pl.BlockSpec((1, bm, d), (lambda k: (lambda i, j: (i, 0, k)))(k)) for k in range(3)]
        return [pl.BlockSpec((1, bm, d), (lambda k: (lambda i, j: (i, j, k)))(k)) for k in range(3)]
    if tmod == 1:
        return [pl.BlockSpec((1, bm, d), (lambda k: (lambda i, j, e: (i, 0, k)))(k)) for k in range(3)]
    return [pl.BlockSpec((1, bm, d), (lambda k: (lambda i, j, e: (i, j, k)))(k)) for k in range(3)]


def _post(alpha, x3, mod3, oa, ob, oc, wg, wb, wo, ln_g, ln_b, tm):
    s, t, d = x3.shape
    rows = lambda i, j: (i, j, 0)
    full = lambda a: pl.BlockSpec(a.shape, lambda i, j: (0,) * a.ndim)
    return pl.pallas_call(
        functools.partial(_post_kernel, alpha),
        grid=(s, t // tm),
        in_specs=[pl.BlockSpec((1, tm, d), rows)] + _mod_specs(mod3.shape[1], tm, d, 2)
                 + [pl.BlockSpec((1, tm, A_V), rows), pl.BlockSpec((1, tm, B_W), rows),
                    pl.BlockSpec((1, tm, C_W), rows), full(wg), full(wb), full(wo), full(ln_g), full(ln_b)],
        out_specs=pl.BlockSpec((1, tm, d), rows),
        out_shape=jax.ShapeDtypeStruct((s, t, d), F32),
        compiler_params=_cparams(("arbitrary", "arbitrary")),
        name="post",
    )(x3, mod3, mod3, mod3, oa, ob, oc, wg, wb, wo, ln_g, ln_b)


def _ffn_kernel(alpha, n_chunks, x_ref, shift_ref, scale_ref, gate_ref, wi_ref, wo_ref, g_ref, b_ref, y_ref):
    x = x_ref[0]
    u = (x * (1.0 + scale_ref[0]) + shift_ref[0]).astype(BF16)
    ff = wo_ref.shape[0]
    cw = ff // n_chunks
    h = None
    for c in range(n_chunks):
        a = _dot(u, wi_ref[:, c * cw:(c + 1) * cw])
        b = _dot(u, wi_ref[:, ff + c * cw:ff + (c + 1) * cw])
        part = _dot((_silu(a) * b).astype(BF16), wo_ref[c * cw:(c + 1) * cw, :])
        h = part if h is None else h + part
    y_ref[0] = _layer_norm(alpha * x + gate_ref[0] * h, g_ref[...], b_ref[...])


def _ffn(alpha, x3, mod3, wi, wo, ln_g, ln_b, tm):
    s, t, d = x3.shape
    rows = lambda i, j: (i, j, 0)
    full = lambda a: pl.BlockSpec(a.shape, lambda i, j: (0,) * a.ndim)
    return pl.pallas_call(
        functools.partial(_ffn_kernel, alpha, 2),
        grid=(s, t // tm),
        in_specs=[pl.BlockSpec((1, tm, d), rows)] + _mod_specs(mod3.shape[1], tm, d, 2)
                 + [full(wi), full(wo), full(ln_g), full(ln_b)],
        out_specs=pl.BlockSpec((1, tm, d), rows),
        out_shape=jax.ShapeDtypeStruct((s, t, d), F32),
        compiler_params=_cparams(("arbitrary", "arbitrary")),
        name="ffn",
    )(x3, mod3, mod3, mod3, wi, wo, ln_g, ln_b)


def _moe_kernel(alpha, x_ref, shift_ref, scale_ref, gate_ref, wr_ref, br_ref, wi_ref, wo_ref, g_ref, b_ref,
                y_ref, u_ref, we_ref, acc_ref):
    e = pl.program_id(2)
    tm = x_ref.shape[1]
    lane = lax.broadcasted_iota(jnp.int32, (tm, LANES), 1)

    @pl.when(e == 0)
    def _():
        u = x_ref[0] * (1.0 + scale_ref[0]) + shift_ref[0]
        u_ref[...] = u.astype(BF16)
        logits = jnp.dot(u, wr_ref[...], preferred_element_type=F32,
                         precision=lax.Precision.HIGHEST) + br_ref[...]
        real = lane < N_EXPERTS
        logits = jnp.where(real, logits, -jnp.inf)
        ex = jnp.exp(logits - jnp.max(logits, axis=1, keepdims=True))
        probs = jnp.where(real, ex / jnp.sum(ex, axis=1, keepdims=True), -1.0)
        p1 = jnp.max(probs, axis=1, keepdims=True)
        i1 = jnp.min(jnp.where(probs == p1, lane, LANES), axis=1, keepdims=True)
        rest = jnp.where(lane == i1, -1.0, probs)
        p2 = jnp.max(rest, axis=1, keepdims=True)
        i2 = jnp.min(jnp.where(rest == p2, lane, LANES), axis=1, keepdims=True)
        tot = p1 + p2
        we_ref[...] = jnp.where(lane == i1, p1 / tot, jnp.where(lane == i2, p2 / tot, 0.0))
        acc_ref[...] = jnp.zeros_like(acc_ref)

    u = u_ref[...]
    ff = wo_ref.shape[1]
    a = _dot(u, wi_ref[0, :, 0:ff])
    b = _dot(u, wi_ref[0, :, ff:2 * ff])
    h = _dot((_silu(a) * b).astype(BF16), wo_ref[0])
    w_e = jnp.sum(jnp.where(lane == e, we_ref[...], 0.0), axis=1, keepdims=True)
    acc_ref[...] += w_e * h

    @pl.when(e == pl.num_programs(2) - 1)
    def _():
        y_ref[0] = _layer_norm(alpha * x_ref[0] + gate_ref[0] * acc_ref[...], g_ref[...], b_ref[...])


def _moe(alpha, x3, mod3, wr, br, wi, wo, ln_g, ln_b, tm):
    s, t, d = x3.shape
    n_e = wi.shape[0]
    rows = lambda i, j, e: (i, j, 0)
    full = lambda a: pl.BlockSpec(a.shape, lambda i, j, e: (0,) * a.ndim)
    return pl.pallas_call(
        functools.partial(_moe_kernel, alpha),
        grid=(s, t // tm, n_e),
        in_specs=[pl.BlockSpec((1, tm, d), rows)] + _mod_specs(mod3.shape[1], tm, d, 3)
                 + [full(wr), full(br),
                    pl.BlockSpec((1,) + wi.shape[1:], lambda i, j, e: (e, 0, 0)),
                    pl.BlockSpec((1,) + wo.shape[1:], lambda i, j, e: (e, 0, 0)),
                    full(ln_g), full(ln_b)],
        out_specs=pl.BlockSpec((1, tm, d), rows),
        out_shape=jax.ShapeDtypeStruct((s, t, d), F32),
        scratch_shapes=[pltpu.VMEM((tm, d), BF16), pltpu.VMEM((tm, LANES), F32), pltpu.VMEM((tm, d), F32)],
        compiler_params=_cparams(("arbitrary", "arbitrary", "arbitrary")),
        name="moe",
    )(x3, mod3, mod3, mod3, wr, br, wi, wo, ln_g, ln_b)


def _pad_lanes(a, width=LANES, value=0.0):
    return jnp.pad(a, [(0, 0)] * (a.ndim - 1) + [(0, width - a.shape[-1])], constant_values=value)


def kernel(x_prompt, x_sample, cache_kv, cache_logf, page_table, c_prompt, c_sample, w_in, b_forget, lambda_qk,
           subln_gain, w_branch, w_out, w_ada, b_ada, ln_gain, ln_bias, w_ff_in, w_ff_out, w_router, b_router,
           w_exp_in, w_exp_out):
    depth = w_in.shape[0]
    nb, seq, d = x_prompt.shape
    ns, dec_seq, _ = x_sample.shape
    assert dec_seq == 1 and d == D_MODEL
    n_pages = page_table.shape[1]
    assert (n_pages * PAGE) % MOBA_BLOCK == 0
    alpha = (2 * depth) ** 0.25

    c_all = jnp.concatenate([c_prompt, c_sample], axis=0)
    mods = _ada(c_all, w_ada.reshape(depth * 2, d, 3 * d), b_ada.reshape(depth * 2, 3 * d))

    xp = x_prompt
    xs = x_sample.reshape(1, ns, d)
    tm_p = 512
    head_mask = (np.arange(Q_COLS)[:, None] // HEAD_DIM == np.arange(LANES)[None, :])
    kv_p, lf_p, kv_s, lf_s = [], [], [], []
    for l in range(depth):
        lam_init = 0.8 - 0.6 * math.exp(-0.3 * l)
        mod_p = mods[2 * l, :nb].reshape(nb, 1, 3 * d)
        mod_s = mods[2 * l, nb:].reshape(1, ns, 3 * d)
        w_qkv = w_in[l, :, :F_COL].astype(BF16)
        w_f = _pad_lanes(w_in[l, :, F_COL:G_COL]).astype(BF16)
        b_f = _pad_lanes(b_forget[l].reshape(1, HC))
        wg = w_in[l, :, G_COL:].astype(BF16)
        wb = w_branch[l].astype(BF16)
        wo = w_out[l].astype(BF16)
        lam = lambda_qk[l]
        subln = subln_gain[l].reshape(1, LANES)
        g0, b0 = ln_gain[l, 0].reshape(1, d), ln_bias[l, 0].reshape(1, d)
        g1, b1 = ln_gain[l, 1].reshape(1, d), ln_bias[l, 1].reshape(1, d)

        kv, logf, q_bf, kv_bf, qb = _in_proj(xp, mod_p, w_qkv, w_f, b_f, tm_p)
        qx, kx = _prep(logf, kv, qb)
        oa = _attn("diff", q_bf, qx, kv_bf, kx, lam, subln, lam_init)
        ob = _attn("moba", q_bf, qx, kv_bf, kx, lam, subln, lam_init)
        oc = _attn("fox", q_bf, qx, kv_bf, kx, lam, subln, lam_init)
        xp = _post(alpha, xp, mod_p, oa, ob, oc, wg, wb, wo, g0, b0, tm_p)
        kv_p.append(kv)
        lf_p.append(logf)

        kvs, logfs, q_bfs, _, qbs = _in_proj(xs, mod_s, w_qkv, w_f, b_f, ns)
        q_s = q_bfs.reshape(ns, Q_COLS)
        qbd = q_s[:, :, None] * jnp.asarray(head_mask, BF16)[None]
        qrows = q_s.astype(F32)[:, None, :] * jnp.asarray(head_mask.T[:N_ROWS], F32)[None]
        oas, obs, ocs = _decode(l, cache_kv, cache_logf, page_table, qbd, qrows,
                                kvs.reshape(ns, 1, KV_WIDTH), _pad_lanes(logfs.reshape(ns, 1, HC)),
                                qbs.reshape(ns, 1, B_W), lam, subln, lam_init)
        xs = _post(alpha, xs, mod_s, oas.reshape(1, ns, A_V), obs.reshape(1, ns, B_W), ocs.reshape(1, ns, C_W),
                   wg, wb, wo, g0, b0, ns)
        kv_s.append(kvs.reshape(ns, 1, KV_WIDTH))
        lf_s.append(logfs.reshape(ns, 1, HC))

        mod_p = mods[2 * l + 1, :nb].reshape(nb, 1, 3 * d)
        mod_s = mods[2 * l + 1, nb:].reshape(1, ns, 3 * d)
        if l % 2 == 0:
            wi, wo2 = w_ff_in[l // 2].astype(BF16), w_ff_out[l // 2].astype(BF16)
            xp = _ffn(alpha, xp, mod_p, wi, wo2, g1, b1, tm_p)
            xs = _ffn(alpha, xs, mod_s, wi, wo2, g1, b1, ns)
        else:
            wr = _pad_lanes(w_router[l // 2])
            br = _pad_lanes(b_router[l // 2].reshape(1, N_EXPERTS))
            wi, wo2 = w_exp_in[l // 2].astype(BF16), w_exp_out[l // 2].astype(BF16)
            xp = _moe(alpha, xp, mod_p, wr, br, wi, wo2, g1, b1, tm_p)
            xs = _moe(alpha, xs, mod_s, wr, br, wi, wo2, g1, b1, ns)

    return (xp, xs.reshape(ns, 1, d), jnp.stack(kv_p, axis=1), jnp.stack(lf_p, axis=1),
            jnp.stack(kv_s, axis=1), jnp.stack(lf_s, axis=1))
```

```python
import functools
import math

import jax
import jax.numpy as jnp
import numpy as np
from jax import lax
from jax.experimental import pallas as pl
from jax.experimental.pallas import tpu as pltpu

D_MODEL = 1024
HEAD_DIM = 64
HA, HB, HC = 4, 4, 4
A_QK = HA * 2 * HEAD_DIM
A_V = HA * 2 * HEAD_DIM
B_W = HB * HEAD_DIM
C_W = HC * HEAD_DIM
N_BRANCH = 3
Q_COLS = A_QK + B_W + C_W
KV_WIDTH = A_QK + A_V + 2 * B_W + 2 * C_W
F_COL = Q_COLS + KV_WIDTH
G_COL = F_COL + HC
MOBA_BLOCK = 256
MOBA_TOPK = 3
N_EXPERTS = 8
TOP_K = 2
LN_EPS = 1e-5
PAGE = 128

KA0, VA0, KB0, VB0, KC0, VC0 = 0, 512, 1024, 1280, 1536, 1792

LANES = 128
HALF = LANES // 2
VMEM_LIMIT = 56 * 1024 * 1024

N_UNITS = 8
MASK_NEG = -(2.0 ** 100)
KV_UNROLL = 4

F32 = jnp.float32
BF16 = jnp.bfloat16


def _alibi_slopes():
    n = HA + HB
    s = 2.0 ** (-8.0 * (np.arange(n, dtype=np.float64) + 1.0) / n)
    mant = np.frexp(s)[0] * 256.0
    assert np.all(mant == np.round(mant)), "slopes must be exact in bfloat16"
    return s


SLOPES = _alibi_slopes()


def _cparams(sem):
    return pltpu.CompilerParams(dimension_semantics=sem, vmem_limit_bytes=VMEM_LIMIT)


def _sigmoid(x):
    return 1.0 / (1.0 + jnp.exp(-x))


def _silu(x):
    return x * _sigmoid(x)


def _layer_norm(x, g, b):
    mu = jnp.mean(x, axis=-1, keepdims=True)
    xc = x - mu
    var = jnp.mean(xc * xc, axis=-1, keepdims=True)
    return xc * lax.rsqrt(var + LN_EPS) * g + b


def _dot(a, b):
    return jnp.dot(a, b, preferred_element_type=F32)


def _dot_nt(a, b, precision=None):
    return lax.dot_general(a, b, (((1,), (1,)), ((), ())), preferred_element_type=F32, precision=precision)


def _ada_kernel(c_ref, w_ref, b_ref, o_ref):
    c = _silu(c_ref[...]).astype(BF16)
    o_ref[0] = _dot(c, w_ref[0].astype(BF16)) + b_ref[0]


def _ada(c_all, w_ada4, b_ada4):
    n, d = c_all.shape
    k, _, n3 = w_ada4.shape
    tn = 1024
    return pl.pallas_call(
        _ada_kernel,
        grid=(k, n3 // tn),
        in_specs=[pl.BlockSpec((n, d), lambda i, j: (0, 0)),
                  pl.BlockSpec((1, d, tn), lambda i, j: (i, 0, j)),
                  pl.BlockSpec((1, 1, tn), lambda i, j: (i, 0, j))],
        out_specs=pl.BlockSpec((1, n, tn), lambda i, j: (i, 0, j)),
        out_shape=jax.ShapeDtypeStruct((k, n, n3), F32),
        compiler_params=_cparams(("arbitrary", "arbitrary")),
        name="ada",
    )(c_all, w_ada4, b_ada4.reshape(k, 1, n3))


def _in_proj_kernel(n_alias, x_ref, shift_ref, scale_ref, w_ref, wf_ref, bf_ref, *refs):
    kv_ref, logf_ref, qbf_ref, kvbf_ref, qb_ref = refs[n_alias:]
    u = (x_ref[0] * (1.0 + scale_ref[0]) + shift_ref[0]).astype(BF16)
    zq = _dot(u, w_ref[:, 0:Q_COLS]) * (HEAD_DIM ** -0.5)
    qbf_ref[0] = zq.astype(BF16)
    qb_ref[0] = zq[:, A_QK:A_QK + B_W]
    for c in range(KV_WIDTH // 1024):
        lo = c * 1024
        zkv = _dot(u, w_ref[:, Q_COLS + lo:Q_COLS + lo + 1024])
        kv_ref[0, 0, :, lo:lo + 1024] = zkv
        kvbf_ref[0, :, lo:lo + 1024] = zkv.astype(BF16)
    zf = _dot(u, wf_ref[...]) + bf_ref[...]
    ls = jnp.minimum(zf, 0.0) - jnp.log(1.0 + jnp.exp(-jnp.abs(zf)))
    logf_ref[0, 0] = ls[:, 0:HC]


def _in_proj(x3, mod3, w_qkv, w_f, b_f, tm, layer, depth, prev):
    s, t, d = x3.shape
    lrows = lambda i, j: (i, layer, j, 0)
    n_alias = 0 if prev is None else 2
    alias_specs = [pl.BlockSpec(memory_space=pl.ANY)] * n_alias
    alias_args = [] if prev is None else list(prev)
    tmod = mod3.shape[1]
    rows = (lambda i, j: (i, j, 0))
    if tmod == 1:
        mod_map = lambda k: (lambda i, j: (i, 0, k))
    else:
        mod_map = lambda k: (lambda i, j: (i, j, k))
    bm = 1 if tmod == 1 else tm
    return pl.pallas_call(
        functools.partial(_in_proj_kernel, n_alias),
        grid=(s, t // tm),
        in_specs=[pl.BlockSpec((1, tm, d), rows),
                  pl.BlockSpec((1, bm, d), mod_map(0)),
                  pl.BlockSpec((1, bm, d), mod_map(1)),
                  pl.BlockSpec(w_qkv.shape, lambda i, j: (0, 0)),
                  pl.BlockSpec(w_f.shape, lambda i, j: (0, 0)),
                  pl.BlockSpec(b_f.shape, lambda i, j: (0, 0))] + alias_specs,
        out_specs=[pl.BlockSpec((1, 1, tm, KV_WIDTH), lrows),
                   pl.BlockSpec((1, 1, tm, HC), lrows),
                   pl.BlockSpec((1, tm, Q_COLS), rows),
                   pl.BlockSpec((1, tm, KV_WIDTH), rows),
                   pl.BlockSpec((1, tm, B_W), rows)],
        out_shape=[jax.ShapeDtypeStruct((s, depth, t, KV_WIDTH), F32),
                   jax.ShapeDtypeStruct((s, depth, t, HC), F32),
                   jax.ShapeDtypeStruct((s, t, Q_COLS), BF16),
                   jax.ShapeDtypeStruct((s, t, KV_WIDTH), BF16),
                   jax.ShapeDtypeStruct((s, t, B_W), F32)],
        compiler_params=_cparams(("arbitrary", "arbitrary")),
        input_output_aliases={6: 0, 7: 1} if prev is not None else {},
        name="in_proj",
    )(x3, mod3, mod3, w_qkv, w_f, b_f, *alias_args)


def _prep_kernel(n_blocks, logf_ref, kb_ref, qb_ref, qx_ref, kx_ref, carry_ref, km_ref):
    t = pl.program_id(1)
    tp = MOBA_BLOCK
    lane = lax.broadcasted_iota(jnp.int32, (tp, LANES), 1)
    sub = lane % HALF
    row = lax.broadcasted_iota(jnp.int32, (tp, LANES), 0)

    @pl.when(t == 0)
    def _():
        carry_ref[...] = jnp.zeros_like(carry_ref)
        km_ref[...] = jnp.zeros_like(km_ref)

    pos = t * tp + row
    kx_pos = jnp.where(sub == 0, (pos // HALF).astype(F32),
                       jnp.where(sub == 1, (pos % HALF).astype(F32), 0.0))
    for h in range(HA):
        sl = float(SLOPES[h])
        qx_ref[0, h] = jnp.where(sub == 0, sl * HALF, jnp.where(sub == 1, sl, 0.0)).astype(BF16)
        kx_ref[0, h] = kx_pos.astype(BF16)

    lf = logf_ref[0, 0]
    r2 = lax.broadcasted_iota(jnp.int32, (tp, tp), 0)
    c2 = lax.broadcasted_iota(jnp.int32, (tp, tp), 1)
    tri = (c2 <= r2).astype(F32)
    cum = jnp.dot(tri, lf, preferred_element_type=F32, precision=lax.Precision.HIGHEST)
    qx_c = jnp.where(sub < 3, -1.0, 0.0).astype(BF16)
    for j in range(HC // 2):
        kx = jnp.zeros((tp, LANES), F32)
        for e in range(2):
            h = 2 * j + e
            c = cum[:, h:h + 1] + carry_ref[h:h + 1, 0:1]
            carry_ref[h:h + 1, :] = jnp.broadcast_to(c[tp - 1:tp, :], (1, LANES))
            hi = c.astype(BF16).astype(F32)
            r1 = c - hi
            mid = r1.astype(BF16).astype(F32)
            lo = (r1 - mid).astype(BF16).astype(F32)
            base = e * HALF
            kx = jnp.where(lane == base, hi, jnp.where(lane == base + 1, mid,
                                                       jnp.where(lane == base + 2, lo, kx)))
        u = HA + HB // 2 + j
        kx_ref[0, u] = kx.astype(BF16)
        qx_ref[0, u] = qx_c

    kb = kb_ref[0, 0]
    qb = qb_ref[0]
    lane_b = lax.broadcasted_iota(jnp.int32, (tp, B_W), 1)
    nblk = sub - 2
    in_win = (nblk >= 0) & (nblk < HALF - 2)
    past = in_win & (nblk < t)
    for j in range(HB // 2):
        sc = None
        for e in range(2):
            h = 2 * j + e
            qm = jnp.where((lane_b // HEAD_DIM) == h, qb, 0.0)
            s_h = _dot_nt(qm, km_ref[...], precision=lax.Precision.HIGHEST)
            sc = s_h if e == 0 else jnp.where(lane < HALF, sc, s_h)
        sc = jnp.where(past, sc, -jnp.inf)
        rank = jnp.zeros((tp, LANES), jnp.int32)
        for d in range(1, n_blocks):
            lower = pltpu.roll(sc, d, 1)
            upper = pltpu.roll(sc, LANES - d, 1)
            rank = rank + (lower >= sc).astype(jnp.int32) + (upper > sc).astype(jnp.int32)
        keep = past & (rank < MOBA_TOPK)
        sel = jnp.where(past & jnp.logical_not(keep), MASK_NEG, 0.0)
        u = HA + j
        qx = sel
        kx = jnp.where(in_win & (nblk == t), 1.0, kx_pos)
        for e in range(2):
            sl = float(SLOPES[HA + 2 * j + e])
            base = e * HALF
            qx = jnp.where(lane == base, sl * HALF, jnp.where(lane == base + 1, sl, qx))
        qx_ref[0, u] = qx.astype(BF16)
        kx_ref[0, u] = kx.astype(BF16)

    kmean = jnp.sum(kb, axis=0, keepdims=True) * (1.0 / tp)
    km_ref[pl.ds(t + 2, 1), :] = kmean
    km_ref[pl.ds(t + 2 + HALF, 1), :] = kmean


def _prep(logf, kv, qb, layer):
    b, _, t, _ = logf.shape
    tp = MOBA_BLOCK
    nb = t // tp
    assert t % tp == 0 and nb <= HALF // 2
    rows = lambda i, j: (i, j, 0)
    xs = pl.BlockSpec((1, N_UNITS, tp, LANES), lambda i, j: (i, 0, j, 0))
    return pl.pallas_call(
        functools.partial(_prep_kernel, nb),
        grid=(b, nb),
        in_specs=[pl.BlockSpec((1, 1, tp, HC), lambda i, j: (i, layer, j, 0)),
                  pl.BlockSpec((1, 1, tp, B_W), lambda i, j: (i, layer, j, KB0 // B_W)),
                  pl.BlockSpec((1, tp, B_W), rows)],
        out_specs=[xs, xs],
        out_shape=[jax.ShapeDtypeStruct((b, N_UNITS, t, LANES), BF16)] * 2,
        scratch_shapes=[pltpu.VMEM((8, LANES), F32), pltpu.VMEM((LANES, B_W), F32)],
        compiler_params=_cparams(("arbitrary", "arbitrary")),
        name="prep",
    )(logf, kv, qb)


def _lambda(lam_ref, lam_init):
    lq = lam_ref[...]
    d01 = jnp.sum(lq[0:1] * lq[1:2], axis=1, keepdims=True)
    d23 = jnp.sum(lq[2:3] * lq[3:4], axis=1, keepdims=True)
    return jnp.exp(d01) - jnp.exp(d23) + lam_init


def _attn_kernel(mode, tq, lam_init, *refs):
    if mode == "diff":
        q_ref, qx_ref, k_ref, kx_ref, v_ref, lam_ref, subln_ref, o_ref, qs_ref, m_ref, acc_ref, sa_ref, sb_ref = refs
    else:
        q_ref, qx_ref, k_ref, kx_ref, v_ref, o_ref, qs_ref, m_ref, acc_ref, sa_ref, sb_ref = refs
    t = q_ref.shape[1]
    nq = t // tq
    lane2 = lax.broadcasted_iota(jnp.int32, (tq, 2 * LANES), 1)
    first = (lane2 % LANES) < HALF
    ones = jnp.ones((tq, LANES), BF16)
    bufs = (sa_ref, sb_ref)

    def qk(j, dst):
        c0 = pl.multiple_of(j * tq, tq)
        kf = jnp.concatenate([k_ref[0, pl.ds(c0, tq), :], kx_ref[0, 0, pl.ds(c0, tq), :]], axis=1)
        dst[...] = _dot_nt(qs_ref[...], kf)

    def consume(j, src, masked):
        c0 = pl.multiple_of(j * tq, tq)
        vf = jnp.concatenate([v_ref[0, pl.ds(c0, tq), :], ones], axis=1)
        s = src[...]
        if masked:
            row_s = lax.broadcasted_iota(jnp.int32, (2 * tq, tq), 0) % tq
            col_s = lax.broadcasted_iota(jnp.int32, (2 * tq, tq), 1)
            s = jnp.where(col_s <= row_s, s, -jnp.inf)
        m_prev = m_ref[...]
        m_new = jnp.maximum(m_prev, jnp.max(s, axis=1, keepdims=True))
        alpha = jnp.exp(m_prev - m_new)
        p = jnp.exp(s - jnp.concatenate([m_new] * (tq // LANES), axis=1))
        pv = _dot(p.astype(BF16), vf)
        acc_ref[...] = jnp.concatenate([alpha, alpha], axis=1) * acc_ref[...] + pv
        m_ref[...] = m_new

    def run(j0, n_full, with_diag):
        total = n_full + (1 if with_diag else 0)
        for k in range(total):
            if k + 1 < total or not with_diag:
                qk(j0 + k + 1, bufs[(k + 1) % 2])
            consume(j0 + k, bufs[k % 2], with_diag and k == total - 1)

    def q_step(i, carry):
        r0 = pl.multiple_of(i * tq, tq)
        qf = jnp.concatenate([q_ref[0, pl.ds(r0, tq), :], qx_ref[0, 0, pl.ds(r0, tq), :]], axis=1)
        zero = jnp.zeros_like(qf)
        qs_ref[0:tq, :] = jnp.where(first, qf, zero)
        qs_ref[tq:2 * tq, :] = jnp.where(first, zero, qf)
        m_ref[...] = jnp.full_like(m_ref, -jnp.inf)
        acc_ref[...] = jnp.zeros_like(acc_ref)
        qk(0, sa_ref)
        lax.fori_loop(0, i // KV_UNROLL, lambda jj, c: (run(KV_UNROLL * jj, KV_UNROLL, False), c)[1], 0)
        base = (i // KV_UNROLL) * KV_UNROLL
        for r in range(KV_UNROLL):
            @pl.when(i % KV_UNROLL == r)
            def _(r=r):
                run(base, r, True)

        acc = acc_ref[...]
        o = acc[:, 0:LANES] / acc[:, LANES:2 * LANES]
        o1, o2 = o[0:tq], o[tq:2 * tq]
        if mode == "diff":
            od = o1 - _lambda(lam_ref, lam_init) * o2
            od = od * lax.rsqrt(jnp.mean(od * od, axis=-1, keepdims=True) + LN_EPS)
            res = od * subln_ref[...] * (1.0 - lam_init)
        else:
            lane = lax.broadcasted_iota(jnp.int32, (tq, LANES), 1)
            res = jnp.where(lane < HALF, o1, o2)
        o_ref[0, pl.ds(r0, tq), :] = res.astype(o_ref.dtype)
        return carry

    lax.fori_loop(0, nq, q_step, 0)


def _attn(mode, q_bf, qx, kv_bf, kx, lam, subln, lam_init, tq=256):
    b, t, _ = q_bf.shape
    if mode == "diff":
        n_u, u0, q0, k0, v0 = HA, 0, 0, KA0 // LANES, VA0 // LANES
    elif mode == "moba":
        n_u, u0, q0, k0, v0 = HB // 2, HA, A_QK // LANES, KB0 // LANES, VB0 // LANES
    else:
        n_u, u0, q0, k0, v0 = HC // 2, HA + HB // 2, (A_QK + B_W) // LANES, KC0 // LANES, VC0 // LANES
    col = lambda off: pl.BlockSpec((1, t, LANES), lambda i, u: (i, 0, off + u))
    ext = pl.BlockSpec((1, 1, t, LANES), lambda i, u: (i, u0 + u, 0, 0))
    in_specs = [col(q0), ext, col(k0), ext, col(v0)]
    args = [q_bf, qx, kv_bf, kx, kv_bf]
    if mode == "diff":
        in_specs += [pl.BlockSpec(lam.shape, lambda i, u: (0, 0)),
                     pl.BlockSpec(subln.shape, lambda i, u: (0, 0))]
        args += [lam, subln]
    return pl.pallas_call(
        functools.partial(_attn_kernel, mode, tq, lam_init),
        grid=(b, n_u),
        in_specs=in_specs,
        out_specs=pl.BlockSpec((1, t, LANES), lambda i, u: (i, 0, u)),
        out_shape=jax.ShapeDtypeStruct((b, t, n_u * LANES), BF16),
        scratch_shapes=[pltpu.VMEM((2 * tq, 2 * LANES), BF16),
                        pltpu.VMEM((2 * tq, LANES), F32),
                        pltpu.VMEM((2 * tq, 2 * LANES), F32),
                        pltpu.VMEM((2 * tq, tq), F32),
                        pltpu.VMEM((2 * tq, tq), F32)],
        compiler_params=_cparams(("arbitrary", "arbitrary")),
        name="attn_" + mode,
    )(*args)


N_ROWS = 2 * HA + HB + HC
ROW_B = 2 * HA
ROW_C = 2 * HA + HB
VCAT = A_V + B_W + C_W


def _decode_kernel(g_pages, n_pages, lam_init, pt_ref, *refs):
    kv_refs = refs[:g_pages]
    lf_refs = refs[g_pages:2 * g_pages]
    (qbd_ref, qrows_ref, kvnew_ref, lfnew_ref, qb_ref, slane_ref, srows_ref, lam_ref, subln_ref,
     oa_ref, ob_ref, oc_ref,
     m_ref, l_ref, acc_ref, carry_ref, mb_ref, lb_ref, ob_scr, km_ref) = refs[2 * g_pages:]
    g = pl.program_id(1)
    n_steps = n_pages // g_pages
    blk_pages = MOBA_BLOCK // PAGE
    blocks_per_step = g_pages // blk_pages
    n_blocks = n_pages // blk_pages

    @pl.when(g == 0)
    def _():
        m_ref[...] = jnp.full_like(m_ref, -jnp.inf)
        l_ref[...] = jnp.zeros_like(l_ref)
        acc_ref[...] = jnp.zeros_like(acc_ref)
        carry_ref[...] = jnp.zeros_like(carry_ref)

    lane = lax.broadcasted_iota(jnp.int32, (PAGE, LANES), 1)
    row = lax.broadcasted_iota(jnp.int32, (PAGE, LANES), 0)
    tri = (lane <= row).astype(F32)
    slane = slane_ref[...]

    for blk in range(blocks_per_step):
        s_parts, v_parts, kb_sum = [], [], None
        for pi in range(blk_pages):
            i = blk * blk_pages + pi
            pg_ref = kv_refs[i]
            kcat = jnp.concatenate([pg_ref[0, 0, :, KA0:KA0 + A_QK].astype(BF16),
                                    pg_ref[0, 0, :, KB0:KB0 + B_W].astype(BF16),
                                    pg_ref[0, 0, :, KC0:KC0 + C_W].astype(BF16)], axis=1)
            st = _dot(kcat, qbd_ref[0])
            pos = ((g * g_pages + i) * PAGE + row).astype(F32)
            st = st + slane * pos
            lf = lf_refs[i][0, 0]
            cum = jnp.dot(tri, lf, preferred_element_type=F32, precision=lax.Precision.HIGHEST)
            for h in range(HC):
                c = cum[:, h:h + 1] + carry_ref[h:h + 1, 0:1]
                carry_ref[h:h + 1, :] = jnp.broadcast_to(c[PAGE - 1:PAGE, :], (1, LANES))
                st = jnp.where(lane == ROW_C + h, st - c, st)
            s_parts.append(st.T[0:N_ROWS])
            v_parts.append(jnp.concatenate([pg_ref[0, 0, :, VA0:VA0 + A_V].astype(BF16),
                                            pg_ref[0, 0, :, VB0:VB0 + B_W].astype(BF16),
                                            pg_ref[0, 0, :, VC0:VC0 + C_W].astype(BF16)], axis=1))
            ks = jnp.sum(pg_ref[0, 0, :, KB0:KB0 + B_W], axis=0, keepdims=True)
            kb_sum = ks if kb_sum is None else kb_sum + ks
        s = jnp.concatenate(s_parts, axis=1)
        v = jnp.concatenate(v_parts, axis=0)
        m_blk = jnp.max(s, axis=1, keepdims=True)
        p = jnp.exp(s - m_blk)
        l_blk = jnp.sum(p, axis=1, keepdims=True)
        o_blk = _dot(p.astype(BF16), v)
        m_prev = m_ref[...]
        m_new = jnp.maximum(m_prev, m_blk)
        a_prev = jnp.exp(m_prev - m_new)
        a_blk = jnp.exp(m_blk - m_new)
        l_ref[...] = a_prev * l_ref[...] + a_blk * l_blk
        acc_ref[...] = a_prev * acc_ref[...] + a_blk * o_blk
        m_ref[...] = m_new
        n = g * blocks_per_step + blk
        for h in range(HB):
            r = ROW_B + h
            mb_ref[h, pl.ds(n, 1), :] = jnp.broadcast_to(m_blk[r:r + 1, :], (1, LANES))
            lb_ref[h, pl.ds(n, 1), :] = jnp.broadcast_to(l_blk[r:r + 1, :], (1, LANES))
            ob_scr[h, pl.ds(n, 1), :] = o_blk[r:r + 1, A_V:A_V + B_W]
        km_ref[pl.ds(n, 1), :] = kb_sum * (1.0 / MOBA_BLOCK)

    @pl.when(g == n_steps - 1)
    def _():
        kvn = kvnew_ref[0]
        kn = jnp.concatenate([kvn[:, KA0:KA0 + A_QK], kvn[:, KB0:KB0 + B_W], kvn[:, KC0:KC0 + C_W]], axis=1)
        vn = jnp.concatenate([kvn[:, VA0:VA0 + A_V], kvn[:, VB0:VB0 + B_W], kvn[:, VC0:VC0 + C_W]], axis=1)
        s_self = jnp.sum(qrows_ref[0] * kn, axis=1, keepdims=True)
        s_self = s_self + srows_ref[:, 0:1] * float(n_pages * PAGE)
        lfn = lfnew_ref[0]
        m_all, l_all, acc_all = m_ref[...], l_ref[...], acc_ref[...]

        def finish(r, s_r, lo, width):
            m_r = m_all[r:r + 1, :]
            m_f = jnp.maximum(m_r, s_r)
            a = jnp.exp(m_r - m_f)
            e = jnp.exp(s_r - m_f)
            return (a * acc_all[r:r + 1, lo:lo + width] + e * vn[:, lo:lo + width]) / (a * l_all[r:r + 1, :] + e)

        lam = _lambda(lam_ref, lam_init)
        for h in range(HA):
            o1 = finish(2 * h, s_self[2 * h:2 * h + 1, :], h * LANES, LANES)
            o2 = finish(2 * h + 1, s_self[2 * h + 1:2 * h + 2, :], h * LANES, LANES)
            od = o1 - lam * o2
            od = od * lax.rsqrt(jnp.mean(od * od, axis=-1, keepdims=True) + LN_EPS)
            oa_ref[0, h:h + 1, :] = od * subln_ref[...] * (1.0 - lam_init)
        for h in range(HC):
            r = ROW_C + h
            s_r = s_self[r:r + 1, :] - (carry_ref[h:h + 1, 0:1] + lfn[:, h:h + 1])
            oc_ref[0, h:h + 1, :] = finish(r, s_r, A_V + B_W + h * HEAD_DIM, HEAD_DIM)
        rowb = lax.broadcasted_iota(jnp.int32, (n_blocks, LANES), 0)
        for h in range(HB):
            r = ROW_B + h
            seg = slice(h * HEAD_DIM, (h + 1) * HEAD_DIM)
            sc = jnp.sum(km_ref[:, seg] * qb_ref[0][:, seg], axis=1, keepdims=True)
            sc = jnp.broadcast_to(sc, (n_blocks, LANES))
            keep = jnp.zeros((n_blocks, LANES), jnp.bool_)
            for _ in range(min(MOBA_TOPK, n_blocks)):
                mx = jnp.max(sc, axis=0, keepdims=True)
                idx = jnp.min(jnp.where(sc == mx, rowb, n_blocks), axis=0, keepdims=True)
                hit = rowb == idx
                keep = keep | hit
                sc = jnp.where(hit, -jnp.inf, sc)
            s_r = s_self[r:r + 1, :]
            mb = jnp.where(keep, mb_ref[h], -jnp.inf)
            m_f = jnp.maximum(jnp.max(mb, axis=0, keepdims=True), s_r)
            w = jnp.where(keep, jnp.exp(mb - m_f), 0.0)
            e = jnp.exp(s_r - m_f)
            l_f = jnp.sum(w * lb_ref[h], axis=0, keepdims=True) + e
            o_f = jnp.sum(w[:, 0:HEAD_DIM] * ob_scr[h][:, seg], axis=0, keepdims=True)
            lo = A_V + h * HEAD_DIM
            ob_ref[0, h:h + 1, :] = (o_f + e[:, 0:HEAD_DIM] * vn[:, lo:lo + HEAD_DIM]) / l_f[:, 0:HEAD_DIM]


def _decode(layer, cache_kv, cache_logf, page_table, qbd, qrows, kvnew, lfnew, qb, lam, subln, lam_init,
            g_pages=8):
    nseq, n_pages = page_table.shape
    assert n_pages % g_pages == 0 and g_pages % (MOBA_BLOCK // PAGE) == 0
    n_blocks = n_pages * PAGE // MOBA_BLOCK
    slane = np.zeros((1, LANES), np.float32)
    srows = np.zeros((N_ROWS, LANES), np.float32)
    for r in range(2 * HA):
        slane[0, r] = SLOPES[r // 2]
        srows[r, :] = SLOPES[r // 2]
    for h in range(HB):
        slane[0, ROW_B + h] = SLOPES[HA + h]
        srows[ROW_B + h, :] = SLOPES[HA + h]

    def page_spec(i, width):
        return pl.BlockSpec((1, 1, PAGE, width), lambda b, g, pt: (pt[b, g * g_pages + i], layer, 0, 0))

    per_seq = lambda shape: pl.BlockSpec((1,) + shape, lambda b, g, pt: (b, 0, 0))
    const = lambda a: pl.BlockSpec(a.shape, lambda b, g, pt: (0, 0))
    in_specs = ([page_spec(i, KV_WIDTH) for i in range(g_pages)]
                + [page_spec(i, HC) for i in range(g_pages)]
                + [per_seq((Q_COLS, LANES)), per_seq((N_ROWS, Q_COLS)), per_seq((1, KV_WIDTH)),
                   per_seq((1, LANES)), per_seq((1, B_W)),
                   const(slane), const(srows), const(lam), const(subln)])
    grid_spec = pltpu.PrefetchScalarGridSpec(
        num_scalar_prefetch=1,
        grid=(nseq, n_pages // g_pages),
        in_specs=in_specs,
        out_specs=[per_seq((HA, LANES)), per_seq((HB, HEAD_DIM)), per_seq((HC, HEAD_DIM))],
        scratch_shapes=[pltpu.VMEM((N_ROWS, 1), F32), pltpu.VMEM((N_ROWS, 1), F32),
                        pltpu.VMEM((N_ROWS, VCAT), F32), pltpu.VMEM((8, LANES), F32),
                        pltpu.VMEM((HB, n_blocks, LANES), F32), pltpu.VMEM((HB, n_blocks, LANES), F32),
                        pltpu.VMEM((HB, n_blocks, B_W), F32), pltpu.VMEM((n_blocks, B_W), F32)])
    return pl.pallas_call(
        functools.partial(_decode_kernel, g_pages, n_pages, lam_init),
        grid_spec=grid_spec,
        out_shape=[jax.ShapeDtypeStruct((nseq, HA, LANES), F32),
                   jax.ShapeDtypeStruct((nseq, HB, HEAD_DIM), F32),
                   jax.ShapeDtypeStruct((nseq, HC, HEAD_DIM), F32)],
        compiler_params=_cparams(("arbitrary", "arbitrary")),
        name="decode",
    )(page_table, *([cache_kv] * g_pages), *([cache_logf] * g_pages),
      qbd, qrows, kvnew, lfnew, qb, jnp.asarray(slane), jnp.asarray(srows), lam, subln)


def _post_kernel(alpha, x_ref, shift_ref, scale_ref, gate_ref, oa_ref, ob_ref, oc_ref,
                 wg_ref, wb_ref, wo_ref, g_ref, b_ref, y_ref):
    x = x_ref[0]
    u = (x * (1.0 + scale_ref[0]) + shift_ref[0]).astype(BF16)
    d = D_MODEL
    merged = _sigmoid(_dot(u, wg_ref[:, 0:d])) * _dot(oa_ref[0].astype(BF16), wb_ref[0:A_V, :])
    merged += _sigmoid(_dot(u, wg_ref[:, d:2 * d])) * _dot(ob_ref[0].astype(BF16), wb_ref[A_V:A_V + B_W, :])
    merged += _sigmoid(_dot(u, wg_ref[:, 2 * d:3 * d])) * _dot(oc_ref[0].astype(BF16), wb_ref[A_V + B_W:, :])
    h = _dot(merged.astype(BF16), wo_ref[...])
    y_ref[0] = _layer_norm(alpha * x + gate_ref[0] * h, g_ref[...], b_ref[...])


def _mod_specs(tmod, tm, d, n_grid):
    bm = 1 if tmod == 1 else tm
    if n_grid == 2:
        if tmod == 1:
            return [pl.BlockSpec((1, bm, d), (lambda k: (lambda i, j: (i, 0, k)))(k)) for k in range(3)]
        return [pl.BlockSpec((1, bm, d), (lambda k: (lambda i, j: (i, j, k)))(k)) for k in range(3)]
    if tmod == 1:
        return [pl.BlockSpec((1, bm, d), (lambda k: (lambda i, j, e: (i, 0, k)))(k)) for k in range(3)]
    return [pl.BlockSpec((1, bm, d), (lambda k: (lambda i, j, e: (i, j, k)))(k)) for k in range(3)]


def _post(alpha, x3, mod3, oa, ob, oc, wg, wb, wo, ln_g, ln_b, tm):
    s, t, d = x3.shape
    rows = lambda i, j: (i, j, 0)
    full = lambda a: pl.BlockSpec(a.shape, lambda i, j: (0,) * a.ndim)
    return pl.pallas_call(
        functools.partial(_post_kernel, alpha),
        grid=(s, t // tm),
        in_specs=[pl.BlockSpec((1, tm, d), rows)] + _mod_specs(mod3.shape[1], tm, d, 2)
                 + [pl.BlockSpec((1, tm, A_V), rows), pl.BlockSpec((1, tm, B_W), rows),
                    pl.BlockSpec((1, tm, C_W), rows), full(wg), full(wb), full(wo), full(ln_g), full(ln_b)],
        out_specs=pl.BlockSpec((1, tm, d), rows),
        out_shape=jax.ShapeDtypeStruct((s, t, d), F32),
        compiler_params=_cparams(("arbitrary", "arbitrary")),
        name="post",
    )(x3, mod3, mod3, mod3, oa, ob, oc, wg, wb, wo, ln_g, ln_b)


def _ffn_kernel(alpha, n_chunks, x_ref, shift_ref, scale_ref, gate_ref, wi_ref, wo_ref, g_ref, b_ref, y_ref):
    x = x_ref[0]
    u = (x * (1.0 + scale_ref[0]) + shift_ref[0]).astype(BF16)
    ff = wo_ref.shape[0]
    cw = ff // n_chunks
    h = None
    for c in range(n_chunks):
        a = _dot(u, wi_ref[:, c * cw:(c + 1) * cw])
        b = _dot(u, wi_ref[:, ff + c * cw:ff + (c + 1) * cw])
        part = _dot((_silu(a) * b).astype(BF16), wo_ref[c * cw:(c + 1) * cw, :])
        h = part if h is None else h + part
    y_ref[0] = _layer_norm(alpha * x + gate_ref[0] * h, g_ref[...], b_ref[...])


def _ffn(alpha, x3, mod3, wi, wo, ln_g, ln_b, tm):
    s, t, d = x3.shape
    rows = lambda i, j: (i, j, 0)
    full = lambda a: pl.BlockSpec(a.shape, lambda i, j: (0,) * a.ndim)
    return pl.pallas_call(
        functools.partial(_ffn_kernel, alpha, 2),
        grid=(s, t // tm),
        in_specs=[pl.BlockSpec((1, tm, d), rows)] + _mod_specs(mod3.shape[1], tm, d, 2)
                 + [full(wi), full(wo), full(ln_g), full(ln_b)],
        out_specs=pl.BlockSpec((1, tm, d), rows),
        out_shape=jax.ShapeDtypeStruct((s, t, d), F32),
        compiler_params=_cparams(("arbitrary", "arbitrary")),
        name="ffn",
    )(x3, mod3, mod3, mod3, wi, wo, ln_g, ln_b)


def _moe_kernel(alpha, x_ref, shift_ref, scale_ref, gate_ref, wr_ref, br_ref, wi_ref, wo_ref, g_ref, b_ref,
                y_ref, u_ref, we_ref, acc_ref):
    e = pl.program_id(2)
    tm = x_ref.shape[1]
    lane = lax.broadcasted_iota(jnp.int32, (tm, LANES), 1)

    @pl.when(e == 0)
    def _():
        u = x_ref[0] * (1.0 + scale_ref[0]) + shift_ref[0]
        u_ref[...] = u.astype(BF16)
        logits = jnp.dot(u, wr_ref[...], preferred_element_type=F32,
                         precision=lax.Precision.HIGHEST) + br_ref[...]
        real = lane < N_EXPERTS
        logits = jnp.where(real, logits, -jnp.inf)
        ex = jnp.exp(logits - jnp.max(logits, axis=1, keepdims=True))
        probs = jnp.where(real, ex / jnp.sum(ex, axis=1, keepdims=True), -1.0)
        p1 = jnp.max(probs, axis=1, keepdims=True)
        i1 = jnp.min(jnp.where(probs == p1, lane, LANES), axis=1, keepdims=True)
        rest = jnp.where(lane == i1, -1.0, probs)
        p2 = jnp.max(rest, axis=1, keepdims=True)
        i2 = jnp.min(jnp.where(rest == p2, lane, LANES), axis=1, keepdims=True)
        tot = p1 + p2
        we_ref[...] = jnp.where(lane == i1, p1 / tot, jnp.where(lane == i2, p2 / tot, 0.0))
        acc_ref[...] = jnp.zeros_like(acc_ref)

    u = u_ref[...]
    ff = wo_ref.shape[1]
    a = _dot(u, wi_ref[0, :, 0:ff])
    b = _dot(u, wi_ref[0, :, ff:2 * ff])
    h = _dot((_silu(a) * b).astype(BF16), wo_ref[0])
    w_e = jnp.sum(jnp.where(lane == e, we_ref[...], 0.0), axis=1, keepdims=True)
    acc_ref[...] += w_e * h

    @pl.when(e == pl.num_programs(2) - 1)
    def _():
        y_ref[0] = _layer_norm(alpha * x_ref[0] + gate_ref[0] * acc_ref[...], g_ref[...], b_ref[...])


def _moe(alpha, x3, mod3, wr, br, wi, wo, ln_g, ln_b, tm):
    s, t, d = x3.shape
    n_e = wi.shape[0]
    rows = lambda i, j, e: (i, j, 0)
    full = lambda a: pl.BlockSpec(a.shape, lambda i, j, e: (0,) * a.ndim)
    return pl.pallas_call(
        functools.partial(_moe_kernel, alpha),
        grid=(s, t // tm, n_e),
        in_specs=[pl.BlockSpec((1, tm, d), rows)] + _mod_specs(mod3.shape[1], tm, d, 3)
                 + [full(wr), full(br),
                    pl.BlockSpec((1,) + wi.shape[1:], lambda i, j, e: (e, 0, 0)),
                    pl.BlockSpec((1,) + wo.shape[1:], lambda i, j, e: (e, 0, 0)),
                    full(ln_g), full(ln_b)],
        out_specs=pl.BlockSpec((1, tm, d), rows),
        out_shape=jax.ShapeDtypeStruct((s, t, d), F32),
        scratch_shapes=[pltpu.VMEM((tm, d), BF16), pltpu.VMEM((tm, LANES), F32), pltpu.VMEM((tm, d), F32)],
        compiler_params=_cparams(("arbitrary", "arbitrary", "arbitrary")),
        name="moe",
    )(x3, mod3, mod3, mod3, wr, br, wi, wo, ln_g, ln_b)


def _pad_lanes(a, width=LANES, value=0.0):
    return jnp.pad(a, [(0, 0)] * (a.ndim - 1) + [(0, width - a.shape[-1])], constant_values=value)


def kernel(x_prompt, x_sample, cache_kv, cache_logf, page_table, c_prompt, c_sample, w_in, b_forget, lambda_qk,
           subln_gain, w_branch, w_out, w_ada, b_ada, ln_gain, ln_bias, w_ff_in, w_ff_out, w_router, b_router,
           w_exp_in, w_exp_out):
    depth = w_in.shape[0]
    nb, seq, d = x_prompt.shape
    ns, dec_seq, _ = x_sample.shape
    assert dec_seq == 1 and d == D_MODEL
    n_pages = page_table.shape[1]
    assert (n_pages * PAGE) % MOBA_BLOCK == 0
    alpha = (2 * depth) ** 0.25

    c_all = jnp.concatenate([c_prompt, c_sample], axis=0)
    mods = _ada(c_all, w_ada.reshape(depth * 2, d, 3 * d), b_ada.reshape(depth * 2, 3 * d))

    xp = x_prompt
    xs = x_sample.reshape(1, ns, d)
    tm_p = 512
    head_mask = (np.arange(Q_COLS)[:, None] // HEAD_DIM == np.arange(LANES)[None, :])
    kv_p = lf_p = kv_s = lf_s = None
    for l in range(depth):
        lam_init = 0.8 - 0.6 * math.exp(-0.3 * l)
        mod_p = mods[2 * l, :nb].reshape(nb, 1, 3 * d)
        mod_s = mods[2 * l, nb:].reshape(1, ns, 3 * d)
        w_qkv = w_in[l, :, :F_COL].astype(BF16)
        w_f = _pad_lanes(w_in[l, :, F_COL:G_COL]).astype(BF16)
        b_f = _pad_lanes(b_forget[l].reshape(1, HC))
        wg = w_in[l, :, G_COL:].astype(BF16)
        wb = w_branch[l].astype(BF16)
        wo = w_out[l].astype(BF16)
        lam = lambda_qk[l]
        subln = subln_gain[l].reshape(1, LANES)
        g0, b0 = ln_gain[l, 0].reshape(1, d), ln_bias[l, 0].reshape(1, d)
        g1, b1 = ln_gain[l, 1].reshape(1, d), ln_bias[l, 1].reshape(1, d)

        kv_p, lf_p, q_bf, kv_bf, qb = _in_proj(xp, mod_p, w_qkv, w_f, b_f, tm_p, l, depth,
                                               None if l == 0 else (kv_p, lf_p))
        qx, kx = _prep(lf_p, kv_p, qb, l)
        oa = _attn("diff", q_bf, qx, kv_bf, kx, lam, subln, lam_init)
        ob = _attn("moba", q_bf, qx, kv_bf, kx, lam, subln, lam_init)
        oc = _attn("fox", q_bf, qx, kv_bf, kx, lam, subln, lam_init)
        xp = _post(alpha, xp, mod_p, oa, ob, oc, wg, wb, wo, g0, b0, tm_p)

        kv_s, lf_s, q_bfs, _, qbs = _in_proj(xs, mod_s, w_qkv, w_f, b_f, ns, l, depth,
                                             None if l == 0 else (kv_s, lf_s))
        kvs, logfs = kv_s[0, l], lf_s[0, l]
        q_s = q_bfs.reshape(ns, Q_COLS)
        qbd = q_s[:, :, None] * jnp.asarray(head_mask, BF16)[None]
        qrows = q_s.astype(F32)[:, None, :] * jnp.asarray(head_mask.T[:N_ROWS], F32)[None]
        oas, obs, ocs = _decode(l, cache_kv, cache_logf, page_table, qbd, qrows,
                                kvs.reshape(ns, 1, KV_WIDTH), _pad_lanes(logfs.reshape(ns, 1, HC)),
                                qbs.reshape(ns, 1, B_W), lam, subln, lam_init)
        xs = _post(alpha, xs, mod_s, oas.reshape(1, ns, A_V), obs.reshape(1, ns, B_W), ocs.reshape(1, ns, C_W),
                   wg, wb, wo, g0, b0, ns)

        mod_p = mods[2 * l + 1, :nb].reshape(nb, 1, 3 * d)
        mod_s = mods[2 * l + 1, nb:].reshape(1, ns, 3 * d)
        if l % 2 == 0:
            wi, wo2 = w_ff_in[l // 2].astype(BF16), w_ff_out[l // 2].astype(BF16)
            xp = _ffn(alpha, xp, mod_p, wi, wo2, g1, b1, tm_p)
            xs = _ffn(alpha, xs, mod_s, wi, wo2, g1, b1, ns)
        else:
            wr = _pad_lanes(w_router[l // 2])
            br = _pad_lanes(b_router[l // 2].reshape(1, N_EXPERTS))
            wi, wo2 = w_exp_in[l // 2].astype(BF16), w_exp_out[l // 2].astype(BF16)
            xp = _moe(alpha, xp, mod_p, wr, br, wi, wo2, g1, b1, tm_p)
            xs = _moe(alpha, xs, mod_s, wr, br, wi, wo2, g1, b1, ns)

    return (xp, xs.reshape(ns, 1, d), kv_p, lf_p,
            jnp.transpose(kv_s[0], (1, 0, 2))[:, :, None, :], jnp.transpose(lf_s[0], (1, 0, 2))[:, :, None, :])
```

```python
import functools
import math

import jax
import jax.numpy as jnp
import numpy as np
from jax import lax
from jax.experimental import pallas as pl
from jax.experimental.pallas import tpu as pltpu

D_MODEL = 1024
HEAD_DIM = 64
HA, HB, HC = 4, 4, 4
A_QK = HA * 2 * HEAD_DIM
A_V = HA * 2 * HEAD_DIM
B_W = HB * HEAD_DIM
C_W = HC * HEAD_DIM
N_BRANCH = 3
Q_COLS = A_QK + B_W + C_W
KV_WIDTH = A_QK + A_V + 2 * B_W + 2 * C_W
F_COL = Q_COLS + KV_WIDTH
G_COL = F_COL + HC
MOBA_BLOCK = 256
MOBA_TOPK = 3
N_EXPERTS = 8
TOP_K = 2
LN_EPS = 1e-5
PAGE = 128

KA0, VA0, KB0, VB0, KC0, VC0 = 0, 512, 1024, 1280, 1536, 1792

LANES = 128
HALF = LANES // 2
VMEM_LIMIT = 56 * 1024 * 1024

N_UNITS = 8
MASK_NEG = -(2.0 ** 100)
KV_UNROLL = 2

F32 = jnp.float32
BF16 = jnp.bfloat16


def _alibi_slopes():
    n = HA + HB
    s = 2.0 ** (-8.0 * (np.arange(n, dtype=np.float64) + 1.0) / n)
    mant = np.frexp(s)[0] * 256.0
    assert np.all(mant == np.round(mant)), "slopes must be exact in bfloat16"
    return s


SLOPES = _alibi_slopes()


def _cparams(sem):
    return pltpu.CompilerParams(dimension_semantics=sem, vmem_limit_bytes=VMEM_LIMIT)


def _sigmoid(x):
    return 1.0 / (1.0 + jnp.exp(-x))


def _silu(x):
    return x * _sigmoid(x)


def _layer_norm(x, g, b):
    mu = jnp.mean(x, axis=-1, keepdims=True)
    xc = x - mu
    var = jnp.mean(xc * xc, axis=-1, keepdims=True)
    return xc * lax.rsqrt(var + LN_EPS) * g + b


def _dot(a, b):
    return jnp.dot(a, b, preferred_element_type=F32)


def _dot_nt(a, b, precision=None):
    return lax.dot_general(a, b, (((1,), (1,)), ((), ())), preferred_element_type=F32, precision=precision)


def _ada_kernel(c_ref, w_ref, b_ref, o_ref):
    c = _silu(c_ref[...]).astype(BF16)
    o_ref[0] = _dot(c, w_ref[0].astype(BF16)) + b_ref[0]


def _ada(c_all, w_ada4, b_ada4):
    n, d = c_all.shape
    k, _, n3 = w_ada4.shape
    tn = 1024
    return pl.pallas_call(
        _ada_kernel,
        grid=(k, n3 // tn),
        in_specs=[pl.BlockSpec((n, d), lambda i, j: (0, 0)),
                  pl.BlockSpec((1, d, tn), lambda i, j: (i, 0, j)),
                  pl.BlockSpec((1, 1, tn), lambda i, j: (i, 0, j))],
        out_specs=pl.BlockSpec((1, n, tn), lambda i, j: (i, 0, j)),
        out_shape=jax.ShapeDtypeStruct((k, n, n3), F32),
        compiler_params=_cparams(("arbitrary", "arbitrary")),
        name="ada",
    )(c_all, w_ada4, b_ada4.reshape(k, 1, n3))


def _in_proj_kernel(n_alias, x_ref, shift_ref, scale_ref, w_ref, wf_ref, bf_ref, *refs):
    kv_ref, logf_ref, qbf_ref, kvbf_ref, qb_ref = refs[n_alias:]
    u = (x_ref[0] * (1.0 + scale_ref[0]) + shift_ref[0]).astype(BF16)
    zq = _dot(u, w_ref[:, 0:Q_COLS]) * (HEAD_DIM ** -0.5)
    qbf_ref[0] = zq.astype(BF16)
    qb_ref[0] = zq[:, A_QK:A_QK + B_W]
    for c in range(KV_WIDTH // 1024):
        lo = c * 1024
        zkv = _dot(u, w_ref[:, Q_COLS + lo:Q_COLS + lo + 1024])
        kv_ref[0, 0, :, lo:lo + 1024] = zkv
        kvbf_ref[0, :, lo:lo + 1024] = zkv.astype(BF16)
    zf = _dot(u, wf_ref[...]) + bf_ref[...]
    ls = jnp.minimum(zf, 0.0) - jnp.log(1.0 + jnp.exp(-jnp.abs(zf)))
    logf_ref[0, 0] = ls[:, 0:HC]


def _in_proj(x3, mod3, w_qkv, w_f, b_f, tm, layer, depth, prev):
    s, t, d = x3.shape
    lrows = lambda i, j: (i, layer, j, 0)
    n_alias = 0 if prev is None else 2
    alias_specs = [pl.BlockSpec(memory_space=pl.ANY)] * n_alias
    alias_args = [] if prev is None else list(prev)
    tmod = mod3.shape[1]
    rows = (lambda i, j: (i, j, 0))
    if tmod == 1:
        mod_map = lambda k: (lambda i, j: (i, 0, k))
    else:
        mod_map = lambda k: (lambda i, j: (i, j, k))
    bm = 1 if tmod == 1 else tm
    return pl.pallas_call(
        functools.partial(_in_proj_kernel, n_alias),
        grid=(s, t // tm),
        in_specs=[pl.BlockSpec((1, tm, d), rows),
                  pl.BlockSpec((1, bm, d), mod_map(0)),
                  pl.BlockSpec((1, bm, d), mod_map(1)),
                  pl.BlockSpec(w_qkv.shape, lambda i, j: (0, 0)),
                  pl.BlockSpec(w_f.shape, lambda i, j: (0, 0)),
                  pl.BlockSpec(b_f.shape, lambda i, j: (0, 0))] + alias_specs,
        out_specs=[pl.BlockSpec((1, 1, tm, KV_WIDTH), lrows),
                   pl.BlockSpec((1, 1, tm, HC), lrows),
                   pl.BlockSpec((1, tm, Q_COLS), rows),
                   pl.BlockSpec((1, tm, KV_WIDTH), rows),
                   pl.BlockSpec((1, tm, B_W), rows)],
        out_shape=[jax.ShapeDtypeStruct((s, depth, t, KV_WIDTH), F32),
                   jax.ShapeDtypeStruct((s, depth, t, HC), F32),
                   jax.ShapeDtypeStruct((s, t, Q_COLS), BF16),
                   jax.ShapeDtypeStruct((s, t, KV_WIDTH), BF16),
                   jax.ShapeDtypeStruct((s, t, B_W), F32)],
        compiler_params=_cparams(("arbitrary", "arbitrary")),
        input_output_aliases={6: 0, 7: 1} if prev is not None else {},
        name="in_proj",
    )(x3, mod3, mod3, w_qkv, w_f, b_f, *alias_args)


def _prep_kernel(n_blocks, logf_ref, kb_ref, qb_ref, qx_ref, kx_ref, carry_ref, km_ref):
    t = pl.program_id(1)
    tp = MOBA_BLOCK
    lane = lax.broadcasted_iota(jnp.int32, (tp, LANES), 1)
    sub = lane % HALF
    row = lax.broadcasted_iota(jnp.int32, (tp, LANES), 0)

    @pl.when(t == 0)
    def _():
        carry_ref[...] = jnp.zeros_like(carry_ref)
        km_ref[...] = jnp.zeros_like(km_ref)

    pos = t * tp + row
    kx_pos = jnp.where(sub == 0, (pos // HALF).astype(F32),
                       jnp.where(sub == 1, (pos % HALF).astype(F32), 0.0))
    for h in range(HA):
        sl = float(SLOPES[h])
        qx_ref[0, h] = jnp.where(sub == 0, sl * HALF, jnp.where(sub == 1, sl, 0.0)).astype(BF16)
        kx_ref[0, h] = kx_pos.astype(BF16)

    lf = logf_ref[0, 0]
    r2 = lax.broadcasted_iota(jnp.int32, (tp, tp), 0)
    c2 = lax.broadcasted_iota(jnp.int32, (tp, tp), 1)
    tri = (c2 <= r2).astype(F32)
    cum = jnp.dot(tri, lf, preferred_element_type=F32, precision=lax.Precision.HIGHEST)
    qx_c = jnp.where(sub < 3, -1.0, 0.0).astype(BF16)
    for j in range(HC // 2):
        kx = jnp.zeros((tp, LANES), F32)
        for e in range(2):
            h = 2 * j + e
            c = cum[:, h:h + 1] + carry_ref[h:h + 1, 0:1]
            carry_ref[h:h + 1, :] = jnp.broadcast_to(c[tp - 1:tp, :], (1, LANES))
            hi = c.astype(BF16).astype(F32)
            r1 = c - hi
            mid = r1.astype(BF16).astype(F32)
            lo = (r1 - mid).astype(BF16).astype(F32)
            base = e * HALF
            kx = jnp.where(lane == base, hi, jnp.where(lane == base + 1, mid,
                                                       jnp.where(lane == base + 2, lo, kx)))
        u = HA + HB // 2 + j
        kx_ref[0, u] = kx.astype(BF16)
        qx_ref[0, u] = qx_c

    kb = kb_ref[0, 0]
    qb = qb_ref[0]
    lane_b = lax.broadcasted_iota(jnp.int32, (tp, B_W), 1)
    nblk = sub - 2
    in_win = (nblk >= 0) & (nblk < HALF - 2)
    past = in_win & (nblk < t)
    for j in range(HB // 2):
        sc = None
        for e in range(2):
            h = 2 * j + e
            qm = jnp.where((lane_b // HEAD_DIM) == h, qb, 0.0)
            s_h = _dot_nt(qm, km_ref[...], precision=lax.Precision.HIGHEST)
            sc = s_h if e == 0 else jnp.where(lane < HALF, sc, s_h)
        sc = jnp.where(past, sc, -jnp.inf)
        rank = jnp.zeros((tp, LANES), jnp.int32)
        for d in range(1, n_blocks):
            lower = pltpu.roll(sc, d, 1)
            upper = pltpu.roll(sc, LANES - d, 1)
            rank = rank + (lower >= sc).astype(jnp.int32) + (upper > sc).astype(jnp.int32)
        keep = past & (rank < MOBA_TOPK)
        sel = jnp.where(past & jnp.logical_not(keep), MASK_NEG, 0.0)
        u = HA + j
        qx = sel
        kx = jnp.where(in_win & (nblk == t), 1.0, kx_pos)
        for e in range(2):
            sl = float(SLOPES[HA + 2 * j + e])
            base = e * HALF
            qx = jnp.where(lane == base, sl * HALF, jnp.where(lane == base + 1, sl, qx))
        qx_ref[0, u] = qx.astype(BF16)
        kx_ref[0, u] = kx.astype(BF16)

    kmean = jnp.sum(kb, axis=0, keepdims=True) * (1.0 / tp)
    km_ref[pl.ds(t + 2, 1), :] = kmean
    km_ref[pl.ds(t + 2 + HALF, 1), :] = kmean


def _prep(logf, kv, qb, layer):
    b, _, t, _ = logf.shape
    tp = MOBA_BLOCK
    nb = t // tp
    assert t % tp == 0 and nb <= HALF // 2
    rows = lambda i, j: (i, j, 0)
    xs = pl.BlockSpec((1, N_UNITS, tp, LANES), lambda i, j: (i, 0, j, 0))
    return pl.pallas_call(
        functools.partial(_prep_kernel, nb),
        grid=(b, nb),
        in_specs=[pl.BlockSpec((1, 1, tp, HC), lambda i, j: (i, layer, j, 0)),
                  pl.BlockSpec((1, 1, tp, B_W), lambda i, j: (i, layer, j, KB0 // B_W)),
                  pl.BlockSpec((1, tp, B_W), rows)],
        out_specs=[xs, xs],
        out_shape=[jax.ShapeDtypeStruct((b, N_UNITS, t, LANES), BF16)] * 2,
        scratch_shapes=[pltpu.VMEM((8, LANES), F32), pltpu.VMEM((LANES, B_W), F32)],
        compiler_params=_cparams(("arbitrary", "arbitrary")),
        name="prep",
    )(logf, kv, qb)


def _lambda(lam_ref, lam_init):
    lq = lam_ref[...]
    d01 = jnp.sum(lq[0:1] * lq[1:2], axis=1, keepdims=True)
    d23 = jnp.sum(lq[2:3] * lq[3:4], axis=1, keepdims=True)
    return jnp.exp(d01) - jnp.exp(d23) + lam_init


def _attn_kernel(mode, tq, lam_init, *refs):
    if mode == "diff":
        q_ref, qx_ref, k_ref, kx_ref, v_ref, lam_ref, subln_ref, o_ref, qs_ref, m_ref, acc_ref, sa_ref, sb_ref = refs
    else:
        q_ref, qx_ref, k_ref, kx_ref, v_ref, o_ref, qs_ref, m_ref, acc_ref, sa_ref, sb_ref = refs
    t = q_ref.shape[1]
    nq = t // tq
    lane2 = lax.broadcasted_iota(jnp.int32, (tq, 2 * LANES), 1)
    first = (lane2 % LANES) < HALF
    ones = jnp.ones((tq, LANES), BF16)
    bufs = (sa_ref, sb_ref)

    def qk(j, dst):
        c0 = pl.multiple_of(j * tq, tq)
        kf = jnp.concatenate([k_ref[0, pl.ds(c0, tq), :], kx_ref[0, 0, pl.ds(c0, tq), :]], axis=1)
        dst[...] = _dot_nt(qs_ref[...], kf)

    def consume(j, src, masked):
        c0 = pl.multiple_of(j * tq, tq)
        vf = jnp.concatenate([v_ref[0, pl.ds(c0, tq), :], ones], axis=1)
        s = src[...]
        if masked:
            row_s = lax.broadcasted_iota(jnp.int32, (2 * tq, tq), 0) % tq
            col_s = lax.broadcasted_iota(jnp.int32, (2 * tq, tq), 1)
            s = jnp.where(col_s <= row_s, s, -jnp.inf)
        m_prev = m_ref[...]
        m_new = jnp.maximum(m_prev, jnp.max(s, axis=1, keepdims=True))
        alpha = jnp.exp(m_prev - m_new)
        p = jnp.exp(s - jnp.concatenate([m_new] * (tq // LANES), axis=1))
        pv = _dot(p.astype(BF16), vf)
        acc_ref[...] = jnp.concatenate([alpha, alpha], axis=1) * acc_ref[...] + pv
        m_ref[...] = m_new

    def run(j0, n_full, with_diag):
        total = n_full + (1 if with_diag else 0)
        for k in range(total):
            if k + 1 < total or not with_diag:
                qk(j0 + k + 1, bufs[(k + 1) % 2])
            consume(j0 + k, bufs[k % 2], with_diag and k == total - 1)

    def q_step(i, carry):
        r0 = pl.multiple_of(i * tq, tq)
        qf = jnp.concatenate([q_ref[0, pl.ds(r0, tq), :], qx_ref[0, 0, pl.ds(r0, tq), :]], axis=1)
        zero = jnp.zeros_like(qf)
        qs_ref[0:tq, :] = jnp.where(first, qf, zero)
        qs_ref[tq:2 * tq, :] = jnp.where(first, zero, qf)
        m_ref[...] = jnp.full_like(m_ref, -jnp.inf)
        acc_ref[...] = jnp.zeros_like(acc_ref)
        qk(0, sa_ref)
        lax.fori_loop(0, i // KV_UNROLL, lambda jj, c: (run(KV_UNROLL * jj, KV_UNROLL, False), c)[1], 0)
        base = (i // KV_UNROLL) * KV_UNROLL
        for r in range(KV_UNROLL):
            @pl.when(i % KV_UNROLL == r)
            def _(r=r):
                run(base, r, True)

        acc = acc_ref[...]
        o = acc[:, 0:LANES] / acc[:, LANES:2 * LANES]
        o1, o2 = o[0:tq], o[tq:2 * tq]
        if mode == "diff":
            od = o1 - _lambda(lam_ref, lam_init) * o2
            od = od * lax.rsqrt(jnp.mean(od * od, axis=-1, keepdims=True) + LN_EPS)
            res = od * subln_ref[...] * (1.0 - lam_init)
        else:
            lane = lax.broadcasted_iota(jnp.int32, (tq, LANES), 1)
            res = jnp.where(lane < HALF, o1, o2)
        o_ref[0, pl.ds(r0, tq), :] = res.astype(o_ref.dtype)
        return carry

    lax.fori_loop(0, nq, q_step, 0)


def _attn(mode, q_bf, qx, kv_bf, kx, lam, subln, lam_init, tq=512):
    b, t, _ = q_bf.shape
    if mode == "diff":
        n_u, u0, q0, k0, v0 = HA, 0, 0, KA0 // LANES, VA0 // LANES
    elif mode == "moba":
        n_u, u0, q0, k0, v0 = HB // 2, HA, A_QK // LANES, KB0 // LANES, VB0 // LANES
    else:
        n_u, u0, q0, k0, v0 = HC // 2, HA + HB // 2, (A_QK + B_W) // LANES, KC0 // LANES, VC0 // LANES
    col = lambda off: pl.BlockSpec((1, t, LANES), lambda i, u: (i, 0, off + u))
    ext = pl.BlockSpec((1, 1, t, LANES), lambda i, u: (i, u0 + u, 0, 0))
    in_specs = [col(q0), ext, col(k0), ext, col(v0)]
    args = [q_bf, qx, kv_bf, kx, kv_bf]
    if mode == "diff":
        in_specs += [pl.BlockSpec(lam.shape, lambda i, u: (0, 0)),
                     pl.BlockSpec(subln.shape, lambda i, u: (0, 0))]
        args += [lam, subln]
    return pl.pallas_call(
        functools.partial(_attn_kernel, mode, tq, lam_init),
        grid=(b, n_u),
        in_specs=in_specs,
        out_specs=pl.BlockSpec((1, t, LANES), lambda i, u: (i, 0, u)),
        out_shape=jax.ShapeDtypeStruct((b, t, n_u * LANES), BF16),
        scratch_shapes=[pltpu.VMEM((2 * tq, 2 * LANES), BF16),
                        pltpu.VMEM((2 * tq, LANES), F32),
                        pltpu.VMEM((2 * tq, 2 * LANES), F32),
                        pltpu.VMEM((2 * tq, tq), F32),
                        pltpu.VMEM((2 * tq, tq), F32)],
        compiler_params=_cparams(("arbitrary", "arbitrary")),
        name="attn_" + mode,
    )(*args)


N_ROWS = 2 * HA + HB + HC
ROW_C = 2 * HA
ROW_B = 2 * HA + HC
VCAT = A_V + B_W + C_W


def _decode_kernel(g_pages, n_pages, lam_init, pt_ref, *refs):
    kv_refs = refs[:g_pages]
    lf_refs = refs[g_pages:2 * g_pages]
    (qbd_ref, qrows_ref, kvnew_ref, lfnew_ref, qb_ref, srows_ref, lam_ref, subln_ref,
     oa_ref, ob_ref, oc_ref,
     m_ref, l_ref, acc_ref, carry_ref, mb_ref, lb_ref, ob_scr, km_ref) = refs[2 * g_pages:]
    g = pl.program_id(1)
    n_steps = n_pages // g_pages
    blk_pages = MOBA_BLOCK // PAGE
    blocks_per_step = g_pages // blk_pages
    n_blocks = n_pages // blk_pages

    @pl.when(g == 0)
    def _():
        m_ref[...] = jnp.full_like(m_ref, -jnp.inf)
        l_ref[...] = jnp.zeros_like(l_ref)
        acc_ref[...] = jnp.zeros_like(acc_ref)
        carry_ref[...] = jnp.zeros_like(carry_ref)

    srow = srows_ref[:, 0:1]
    lane_p = lax.broadcasted_iota(jnp.int32, (HC, PAGE), 1)
    lane_b = lax.broadcasted_iota(jnp.int32, (1, MOBA_BLOCK), 1)
    kcat = jnp.concatenate(
        [jnp.concatenate([r[0, 0, :, KA0:KA0 + A_QK].astype(BF16),
                          r[0, 0, :, KC0:KC0 + C_W].astype(BF16),
                          r[0, 0, :, KB0:KB0 + B_W].astype(BF16)], axis=1) for r in kv_refs], axis=0)
    st_all = _dot(kcat, qbd_ref[0])

    for blk in range(blocks_per_step):
        cums, v_parts, kb_sum = [], [], None
        for pi in range(blk_pages):
            i = blk * blk_pages + pi
            pg_ref = kv_refs[i]
            x = lf_refs[i][0, 0].T
            k = 1
            while k < PAGE:
                x = x + jnp.where(lane_p >= k, pltpu.roll(x, k, 1), 0.0)
                k *= 2
            c = x + carry_ref[0:HC, :]
            carry_ref[0:HC, :] = jnp.broadcast_to(c[:, PAGE - 1:PAGE], (HC, LANES))
            cums.append(c)
            v_parts.append(jnp.concatenate([pg_ref[0, 0, :, VA0:VA0 + A_V].astype(BF16),
                                            pg_ref[0, 0, :, VB0:VB0 + B_W].astype(BF16),
                                            pg_ref[0, 0, :, VC0:VC0 + C_W].astype(BF16)], axis=1))
            ks = jnp.sum(pg_ref[0, 0, :, KB0:KB0 + B_W], axis=0, keepdims=True)
            kb_sum = ks if kb_sum is None else kb_sum + ks
        s_t = st_all[blk * MOBA_BLOCK:(blk + 1) * MOBA_BLOCK, :].T
        pos = ((g * g_pages + blk * blk_pages) * PAGE + lane_b).astype(F32)
        s = s_t[0:N_ROWS] + srow * pos
        s = jnp.concatenate([s[0:ROW_C], s[ROW_C:ROW_C + HC] - jnp.concatenate(cums, axis=1),
                             s[ROW_C + HC:N_ROWS]], axis=0)
        v = jnp.concatenate(v_parts, axis=0)
        m_blk = jnp.max(s, axis=1, keepdims=True)
        p = jnp.exp(s - m_blk)
        l_blk = jnp.sum(p, axis=1, keepdims=True)
        o_blk = _dot(p.astype(BF16), v)
        m_prev = m_ref[...]
        m_new = jnp.maximum(m_prev, m_blk)
        a_prev = jnp.exp(m_prev - m_new)
        a_blk = jnp.exp(m_blk - m_new)
        l_ref[...] = a_prev * l_ref[...] + a_blk * l_blk
        acc_ref[...] = a_prev * acc_ref[...] + a_blk * o_blk
        m_ref[...] = m_new
        n = g * blocks_per_step + blk
        for h in range(HB):
            r = ROW_B + h
            mb_ref[h, pl.ds(n, 1), :] = jnp.broadcast_to(m_blk[r:r + 1, :], (1, LANES))
            lb_ref[h, pl.ds(n, 1), :] = jnp.broadcast_to(l_blk[r:r + 1, :], (1, LANES))
            ob_scr[h, pl.ds(n, 1), :] = o_blk[r:r + 1, A_V:A_V + B_W]
        km_ref[pl.ds(n, 1), :] = kb_sum * (1.0 / MOBA_BLOCK)

    @pl.when(g == n_steps - 1)
    def _():
        kvn = kvnew_ref[0]
        kn = jnp.concatenate([kvn[:, KA0:KA0 + A_QK], kvn[:, KC0:KC0 + C_W], kvn[:, KB0:KB0 + B_W]], axis=1)
        vn = jnp.concatenate([kvn[:, VA0:VA0 + A_V], kvn[:, VB0:VB0 + B_W], kvn[:, VC0:VC0 + C_W]], axis=1)
        s_self = jnp.sum(qrows_ref[0] * kn, axis=1, keepdims=True)
        s_self = s_self + srows_ref[:, 0:1] * float(n_pages * PAGE)
        lfn = lfnew_ref[0]
        m_all, l_all, acc_all = m_ref[...], l_ref[...], acc_ref[...]

        def finish(r, s_r, lo, width):
            m_r = m_all[r:r + 1, :]
            m_f = jnp.maximum(m_r, s_r)
            a = jnp.exp(m_r - m_f)
            e = jnp.exp(s_r - m_f)
            return (a * acc_all[r:r + 1, lo:lo + width] + e * vn[:, lo:lo + width]) / (a * l_all[r:r + 1, :] + e)

        lam = _lambda(lam_ref, lam_init)
        for h in range(HA):
            o1 = finish(2 * h, s_self[2 * h:2 * h + 1, :], h * LANES, LANES)
            o2 = finish(2 * h + 1, s_self[2 * h + 1:2 * h + 2, :], h * LANES, LANES)
            od = o1 - lam * o2
            od = od * lax.rsqrt(jnp.mean(od * od, axis=-1, keepdims=True) + LN_EPS)
            oa_ref[0, h:h + 1, :] = od * subln_ref[...] * (1.0 - lam_init)
        for h in range(HC):
            r = ROW_C + h
            s_r = s_self[r:r + 1, :] - (carry_ref[h:h + 1, 0:1] + lfn[:, h:h + 1])
            oc_ref[0, h:h + 1, :] = finish(r, s_r, A_V + B_W + h * HEAD_DIM, HEAD_DIM)
        rowb = lax.broadcasted_iota(jnp.int32, (n_blocks, LANES), 0)
        for h in range(HB):
            r = ROW_B + h
            seg = slice(h * HEAD_DIM, (h + 1) * HEAD_DIM)
            sc = jnp.sum(km_ref[:, seg] * qb_ref[0][:, seg], axis=1, keepdims=True)
            sc = jnp.broadcast_to(sc, (n_blocks, LANES))
            keep = jnp.zeros((n_blocks, LANES), jnp.bool_)
            for _ in range(min(MOBA_TOPK, n_blocks)):
                mx = jnp.max(sc, axis=0, keepdims=True)
                idx = jnp.min(jnp.where(sc == mx, rowb, n_blocks), axis=0, keepdims=True)
                hit = rowb == idx
                keep = keep | hit
                sc = jnp.where(hit, -jnp.inf, sc)
            s_r = s_self[r:r + 1, :]
            mb = jnp.where(keep, mb_ref[h], -jnp.inf)
            m_f = jnp.maximum(jnp.max(mb, axis=0, keepdims=True), s_r)
            w = jnp.where(keep, jnp.exp(mb - m_f), 0.0)
            e = jnp.exp(s_r - m_f)
            l_f = jnp.sum(w * lb_ref[h], axis=0, keepdims=True) + e
            o_f = jnp.sum(w[:, 0:HEAD_DIM] * ob_scr[h][:, seg], axis=0, keepdims=True)
            lo = A_V + h * HEAD_DIM
            ob_ref[0, h:h + 1, :] = (o_f + e[:, 0:HEAD_DIM] * vn[:, lo:lo + HEAD_DIM]) / l_f[:, 0:HEAD_DIM]


def _decode(layer, cache_kv, cache_logf, page_table, qbd, qrows, kvnew, lfnew, qb, lam, subln, lam_init,
            g_pages=8):
    nseq, n_pages = page_table.shape
    assert n_pages % g_pages == 0 and g_pages % (MOBA_BLOCK // PAGE) == 0
    n_blocks = n_pages * PAGE // MOBA_BLOCK
    srows = np.zeros((N_ROWS, LANES), np.float32)
    for r in range(2 * HA):
        srows[r, :] = SLOPES[r // 2]
    for h in range(HB):
        srows[ROW_B + h, :] = SLOPES[HA + h]

    def page_spec(i, width):
        return pl.BlockSpec((1, 1, PAGE, width), lambda b, g, pt: (pt[b, g * g_pages + i], layer, 0, 0))

    per_seq = lambda shape: pl.BlockSpec((1,) + shape, lambda b, g, pt: (b, 0, 0))
    const = lambda a: pl.BlockSpec(a.shape, lambda b, g, pt: (0, 0))
    in_specs = ([page_spec(i, KV_WIDTH) for i in range(g_pages)]
                + [page_spec(i, HC) for i in range(g_pages)]
                + [per_seq((Q_COLS, LANES)), per_seq((N_ROWS, Q_COLS)), per_seq((1, KV_WIDTH)),
                   per_seq((1, LANES)), per_seq((1, B_W)),
                   const(srows), const(lam), const(subln)])
    grid_spec = pltpu.PrefetchScalarGridSpec(
        num_scalar_prefetch=1,
        grid=(nseq, n_pages // g_pages),
        in_specs=in_specs,
        out_specs=[per_seq((HA, LANES)), per_seq((HB, HEAD_DIM)), per_seq((HC, HEAD_DIM))],
        scratch_shapes=[pltpu.VMEM((N_ROWS, 1), F32), pltpu.VMEM((N_ROWS, 1), F32),
                        pltpu.VMEM((N_ROWS, VCAT), F32), pltpu.VMEM((8, LANES), F32),
                        pltpu.VMEM((HB, n_blocks, LANES), F32), pltpu.VMEM((HB, n_blocks, LANES), F32),
                        pltpu.VMEM((HB, n_blocks, B_W), F32), pltpu.VMEM((n_blocks, B_W), F32)])
    return pl.pallas_call(
        functools.partial(_decode_kernel, g_pages, n_pages, lam_init),
        grid_spec=grid_spec,
        out_shape=[jax.ShapeDtypeStruct((nseq, HA, LANES), F32),
                   jax.ShapeDtypeStruct((nseq, HB, HEAD_DIM), F32),
                   jax.ShapeDtypeStruct((nseq, HC, HEAD_DIM), F32)],
        compiler_params=_cparams(("arbitrary", "arbitrary")),
        name="decode",
    )(page_table, *([cache_kv] * g_pages), *([cache_logf] * g_pages),
      qbd, qrows, kvnew, lfnew, qb, jnp.asarray(srows), lam, subln)


def _post_kernel(alpha, x_ref, shift_ref, scale_ref, gate_ref, oa_ref, ob_ref, oc_ref,
                 wg_ref, wb_ref, wo_ref, g_ref, b_ref, y_ref):
    x = x_ref[0]
    u = (x * (1.0 + scale_ref[0]) + shift_ref[0]).astype(BF16)
    d = D_MODEL
    merged = _sigmoid(_dot(u, wg_ref[:, 0:d])) * _dot(oa_ref[0].astype(BF16), wb_ref[0:A_V, :])
    merged += _sigmoid(_dot(u, wg_ref[:, d:2 * d])) * _dot(ob_ref[0].astype(BF16), wb_ref[A_V:A_V + B_W, :])
    merged += _sigmoid(_dot(u, wg_ref[:, 2 * d:3 * d])) * _dot(oc_ref[0].astype(BF16), wb_ref[A_V + B_W:, :])
    h = _dot(merged.astype(BF16), wo_ref[...])
    y_ref[0] = _layer_norm(alpha * x + gate_ref[0] * h, g_ref[...], b_ref[...])


def _mod_specs(tmod, tm, d, n_grid):
    bm = 1 if tmod == 1 else tm
    if n_grid == 2:
        if tmod == 1:
            return [pl.BlockSpec((1, bm, d), (lambda k: (lambda i, j: (i, 0, k)))(k)) for k in range(3)]
        return [pl.BlockSpec((1, bm, d), (lambda k: (lambda i, j: (i, j, k)))(k)) for k in range(3)]
    if tmod == 1:
        return [pl.BlockSpec((1, bm, d), (lambda k: (lambda i, j, e: (i, 0, k)))(k)) for k in range(3)]
    return [pl.BlockSpec((1, bm, d), (lambda k: (lambda i, j, e: (i, j, k)))(k)) for k in range(3)]


def _post(alpha, x3, mod3, oa, ob, oc, wg, wb, wo, ln_g, ln_b, tm):
    s, t, d = x3.shape
    rows = lambda i, j: (i, j, 0)
    full = lambda a: pl.BlockSpec(a.shape, lambda i, j: (0,) * a.ndim)
    return pl.pallas_call(
        functools.partial(_post_kernel, alpha),
        grid=(s, t // tm),
        in_specs=[pl.BlockSpec((1, tm, d), rows)] + _mod_specs(mod3.shape[1], tm, d, 2)
                 + [pl.BlockSpec((1, tm, A_V), rows), pl.BlockSpec((1, tm, B_W), rows),
                    pl.BlockSpec((1, tm, C_W), rows), full(wg), full(wb), full(wo), full(ln_g), full(ln_b)],
        out_specs=pl.BlockSpec((1, tm, d), rows),
        out_shape=jax.ShapeDtypeStruct((s, t, d), F32),
        compiler_params=_cparams(("arbitrary", "arbitrary")),
        name="post",
    )(x3, mod3, mod3, mod3, oa, ob, oc, wg, wb, wo, ln_g, ln_b)


def _ffn_kernel(alpha, n_chunks, x_ref, shift_ref, scale_ref, gate_ref, wi_ref, wo_ref, g_ref, b_ref, y_ref):
    x = x_ref[0]
    u = (x * (1.0 + scale_ref[0]) + shift_ref[0]).astype(BF16)
    ff = wo_ref.shape[0]
    cw = ff // n_chunks
    h = None
    for c in range(n_chunks):
        a = _dot(u, wi_ref[:, c * cw:(c + 1) * cw])
        b = _dot(u, wi_ref[:, ff + c * cw:ff + (c + 1) * cw])
        part = _dot((_silu(a) * b).astype(BF16), wo_ref[c * cw:(c + 1) * cw, :])
        h = part if h is None else h + part
    y_ref[0] = _layer_norm(alpha * x + gate_ref[0] * h, g_ref[...], b_ref[...])


def _ffn(alpha, x3, mod3, wi, wo, ln_g, ln_b, tm):
    s, t, d = x3.shape
    rows = lambda i, j: (i, j, 0)
    full = lambda a: pl.BlockSpec(a.shape, lambda i, j: (0,) * a.ndim)
    return pl.pallas_call(
        functools.partial(_ffn_kernel, alpha, 2),
        grid=(s, t // tm),
        in_specs=[pl.BlockSpec((1, tm, d), rows)] + _mod_specs(mod3.shape[1], tm, d, 2)
                 + [full(wi), full(wo), full(ln_g), full(ln_b)],
        out_specs=pl.BlockSpec((1, tm, d), rows),
        out_shape=jax.ShapeDtypeStruct((s, t, d), F32),
        compiler_params=_cparams(("arbitrary", "arbitrary")),
        name="ffn",
    )(x3, mod3, mod3, mod3, wi, wo, ln_g, ln_b)


def _moe_kernel(alpha, x_ref, shift_ref, scale_ref, gate_ref, wr_ref, br_ref, wi_ref, wo_ref, g_ref, b_ref,
                y_ref, u_ref, we_ref, acc_ref):
    e = pl.program_id(2)
    tm = x_ref.shape[1]
    lane = lax.broadcasted_iota(jnp.int32, (tm, LANES), 1)

    @pl.when(e == 0)
    def _():
        u = x_ref[0] * (1.0 + scale_ref[0]) + shift_ref[0]
        u_ref[...] = u.astype(BF16)
        logits = jnp.dot(u, wr_ref[...], preferred_element_type=F32,
                         precision=lax.Precision.HIGHEST) + br_ref[...]
        real = lane < N_EXPERTS
        logits = jnp.where(real, logits, -jnp.inf)
        ex = jnp.exp(logits - jnp.max(logits, axis=1, keepdims=True))
        probs = jnp.where(real, ex / jnp.sum(ex, axis=1, keepdims=True), -1.0)
        p1 = jnp.max(probs, axis=1, keepdims=True)
        i1 = jnp.min(jnp.where(probs == p1, lane, LANES), axis=1, keepdims=True)
        rest = jnp.where(lane == i1, -1.0, probs)
        p2 = jnp.max(rest, axis=1, keepdims=True)
        i2 = jnp.min(jnp.where(rest == p2, lane, LANES), axis=1, keepdims=True)
        tot = p1 + p2
        we_ref[...] = jnp.where(lane == i1, p1 / tot, jnp.where(lane == i2, p2 / tot, 0.0))
        acc_ref[...] = jnp.zeros_like(acc_ref)

    u = u_ref[...]
    ff = wo_ref.shape[1]
    a = _dot(u, wi_ref[0, :, 0:ff])
    b = _dot(u, wi_ref[0, :, ff:2 * ff])
    h = _dot((_silu(a) * b).astype(BF16), wo_ref[0])
    w_e = jnp.sum(jnp.where(lane == e, we_ref[...], 0.0), axis=1, keepdims=True)
    acc_ref[...] += w_e * h

    @pl.when(e == pl.num_programs(2) - 1)
    def _():
        y_ref[0] = _layer_norm(alpha * x_ref[0] + gate_ref[0] * acc_ref[...], g_ref[...], b_ref[...])


def _moe(alpha, x3, mod3, wr, br, wi, wo, ln_g, ln_b, tm):
    s, t, d = x3.shape
    n_e = wi.shape[0]
    rows = lambda i, j, e: (i, j, 0)
    full = lambda a: pl.BlockSpec(a.shape, lambda i, j, e: (0,) * a.ndim)
    return pl.pallas_call(
        functools.partial(_moe_kernel, alpha),
        grid=(s, t // tm, n_e),
        in_specs=[pl.BlockSpec((1, tm, d), rows)] + _mod_specs(mod3.shape[1], tm, d, 3)
                 + [full(wr), full(br),
                    pl.BlockSpec((1,) + wi.shape[1:], lambda i, j, e: (e, 0, 0)),
                    pl.BlockSpec((1,) + wo.shape[1:], lambda i, j, e: (e, 0, 0)),
                    full(ln_g), full(ln_b)],
        out_specs=pl.BlockSpec((1, tm, d), rows),
        out_shape=jax.ShapeDtypeStruct((s, t, d), F32),
        scratch_shapes=[pltpu.VMEM((tm, d), BF16), pltpu.VMEM((tm, LANES), F32), pltpu.VMEM((tm, d), F32)],
        compiler_params=_cparams(("arbitrary", "arbitrary", "arbitrary")),
        name="moe",
    )(x3, mod3, mod3, mod3, wr, br, wi, wo, ln_g, ln_b)


def _pad_lanes(a, width=LANES, value=0.0):
    return jnp.pad(a, [(0, 0)] * (a.ndim - 1) + [(0, width - a.shape[-1])], constant_values=value)


def kernel(x_prompt, x_sample, cache_kv, cache_logf, page_table, c_prompt, c_sample, w_in, b_forget, lambda_qk,
           subln_gain, w_branch, w_out, w_ada, b_ada, ln_gain, ln_bias, w_ff_in, w_ff_out, w_router, b_router,
           w_exp_in, w_exp_out):
    depth = w_in.shape[0]
    nb, seq, d = x_prompt.shape
    ns, dec_seq, _ = x_sample.shape
    assert dec_seq == 1 and d == D_MODEL
    n_pages = page_table.shape[1]
    assert (n_pages * PAGE) % MOBA_BLOCK == 0
    alpha = (2 * depth) ** 0.25

    c_all = jnp.concatenate([c_prompt, c_sample], axis=0)
    mods = _ada(c_all, w_ada.reshape(depth * 2, d, 3 * d), b_ada.reshape(depth * 2, 3 * d))

    xp = x_prompt
    xs = x_sample.reshape(1, ns, d)
    tm_p = 512
    head_mask = (np.arange(Q_COLS)[:, None] // HEAD_DIM == np.arange(LANES)[None, :])
    kv_p = lf_p = kv_s = lf_s = None
    for l in range(depth):
        lam_init = 0.8 - 0.6 * math.exp(-0.3 * l)
        mod_p = mods[2 * l, :nb].reshape(nb, 1, 3 * d)
        mod_s = mods[2 * l, nb:].reshape(1, ns, 3 * d)
        w_qkv = w_in[l, :, :F_COL].astype(BF16)
        w_f = _pad_lanes(w_in[l, :, F_COL:G_COL]).astype(BF16)
        b_f = _pad_lanes(b_forget[l].reshape(1, HC))
        wg = w_in[l, :, G_COL:].astype(BF16)
        wb = w_branch[l].astype(BF16)
        wo = w_out[l].astype(BF16)
        lam = lambda_qk[l]
        subln = subln_gain[l].reshape(1, LANES)
        g0, b0 = ln_gain[l, 0].reshape(1, d), ln_bias[l, 0].reshape(1, d)
        g1, b1 = ln_gain[l, 1].reshape(1, d), ln_bias[l, 1].reshape(1, d)

        kv_p, lf_p, q_bf, kv_bf, qb = _in_proj(xp, mod_p, w_qkv, w_f, b_f, tm_p, l, depth,
                                               None if l == 0 else (kv_p, lf_p))
        qx, kx = _prep(lf_p, kv_p, qb, l)
        oa = _attn("diff", q_bf, qx, kv_bf, kx, lam, subln, lam_init)
        ob = _attn("moba", q_bf, qx, kv_bf, kx, lam, subln, lam_init)
        oc = _attn("fox", q_bf, qx, kv_bf, kx, lam, subln, lam_init)
        xp = _post(alpha, xp, mod_p, oa, ob, oc, wg, wb, wo, g0, b0, tm_p)

        kv_s, lf_s, q_bfs, _, qbs = _in_proj(xs, mod_s, w_qkv, w_f, b_f, ns, l, depth,
                                             None if l == 0 else (kv_s, lf_s))
        kvs, logfs = kv_s[0, l], lf_s[0, l]
        q_s = q_bfs.reshape(ns, Q_COLS)
        q_s = jnp.concatenate([q_s[:, :A_QK], q_s[:, A_QK + B_W:], q_s[:, A_QK:A_QK + B_W]], axis=1)
        qbd = q_s[:, :, None] * jnp.asarray(head_mask, BF16)[None]
        qrows = q_s.astype(F32)[:, None, :] * jnp.asarray(head_mask.T[:N_ROWS], F32)[None]
        oas, obs, ocs = _decode(l, cache_kv, cache_logf, page_table, qbd, qrows,
                                kvs.reshape(ns, 1, KV_WIDTH), _pad_lanes(logfs.reshape(ns, 1, HC)),
                                qbs.reshape(ns, 1, B_W), lam, subln, lam_init)
        xs = _post(alpha, xs, mod_s, oas.reshape(1, ns, A_V), obs.reshape(1, ns, B_W), ocs.reshape(1, ns, C_W),
                   wg, wb, wo, g0, b0, ns)

        mod_p = mods[2 * l + 1, :nb].reshape(nb, 1, 3 * d)
        mod_s = mods[2 * l + 1, nb:].reshape(1, ns, 3 * d)
        if l % 2 == 0:
            wi, wo2 = w_ff_in[l // 2].astype(BF16), w_ff_out[l // 2].astype(BF16)
            xp = _ffn(alpha, xp, mod_p, wi, wo2, g1, b1, tm_p)
            xs = _ffn(alpha, xs, mod_s, wi, wo2, g1, b1, ns)
        else:
            wr = _pad_lanes(w_router[l // 2])
            br = _pad_lanes(b_router[l // 2].reshape(1, N_EXPERTS))
            wi, wo2 = w_exp_in[l // 2].astype(BF16), w_exp_out[l // 2].astype(BF16)
            xp = _moe(alpha, xp, mod_p, wr, br, wi, wo2, g1, b1, tm_p)
            xs = _moe(alpha, xs, mod_s, wr, br, wi, wo2, g1, b1, ns)

    return (xp, xs.reshape(ns, 1, d), kv_p, lf_p,
            jnp.transpose(kv_s[0], (1, 0, 2))[:, :, None, :], jnp.transpose(lf_s[0], (1, 0, 2))[:, :, None, :])
```
